```python
import math
import jax, jax.numpy as jnp
from jax import lax
import numpy as np

D_MODEL = 1024
BATCH = 8
SEQ = 4096
DEPTH = 2
DEC_BATCH = 32
DEC_SEQ = 1
PAST_LEN = 16384
PAGE_SIZE = 128

N_GLA_LAYERS = (DEPTH + 1) // 2
N_DSA_LAYERS = DEPTH // 2

GLA_HEADS = 4
GLA_DK = D_MODEL // 2 // GLA_HEADS
GLA_DV = D_MODEL // GLA_HEADS
GLA_RANK = 16
GLA_TAU = 16.0
GLA_CHUNK = 32
GLA_IN = 2 * GLA_HEADS * GLA_DK + 2 * GLA_HEADS * GLA_DV + GLA_RANK

ATT_HEADS = 16
KV_HEADS = 4
HEAD_DIM = D_MODEL // ATT_HEADS
IDX_HEADS = 8
IDX_DIM = 64
TOPK_MAX = 256
QBLOCK = 128
ROPE_THETA = 500000.0
ROPE_FRAC = 4
DSA_IN = ATT_HEADS * HEAD_DIM + 2 * KV_HEADS * HEAD_DIM + IDX_HEADS * IDX_DIM + IDX_DIM + IDX_HEADS

D_FF = 2816
N_EXPERTS = 8
TOP_K_EXPERTS = 2
EPS = 1e-6

kernel_name = "hybrid_gla_dsa_decoder_step"


def rmsnorm(x, g):
    xf = x.astype(jnp.float32)
    y = xf * lax.rsqrt(jnp.mean(xf * xf, axis=-1, keepdims=True) + EPS)
    return (y * g.astype(jnp.float32)).astype(x.dtype)


def rope_partial(x, pos):
    d_rot = x.shape[-1] // ROPE_FRAC
    half = d_rot // 2
    inv = ROPE_THETA ** (-jnp.arange(half, dtype=jnp.float32) / half)
    ang = pos.astype(jnp.float32)[:, None] * inv[None, :]
    cos = jnp.cos(ang)[:, None, :]
    sin = jnp.sin(ang)[:, None, :]
    xr = x[..., :d_rot].astype(jnp.float32)
    x1, x2 = xr[..., :half], xr[..., half:]
    rot = jnp.concatenate([x1 * cos - x2 * sin, x2 * cos + x1 * sin], axis=-1)
    return jnp.concatenate([rot.astype(x.dtype), x[..., d_rot:]], axis=-1)


def swiglu(h, w_gate, w_up, w_down):
    return (jax.nn.silu(h @ w_gate) * (h @ w_up)) @ w_down


def moe_ffn(h, w_router, w_gate, w_up, w_down):
    logits = (h @ w_router).astype(jnp.float32)
    top_val, top_idx = lax.top_k(logits, TOP_K_EXPERTS)
    gates = jax.nn.softmax(top_val, axis=-1)
    dense_gate = jnp.sum(jax.nn.one_hot(top_idx, N_EXPERTS, dtype=jnp.float32) * gates[..., None], axis=-2)
    dense_gate = dense_gate.astype(h.dtype)
    out = jnp.zeros_like(h)
    for e in range(N_EXPERTS):
        out = out + dense_gate[..., e:e + 1] * swiglu(h, w_gate[e], w_up[e], w_down[e])
    return out


def gla_scan(q, k, v, log_a, s0):
    B, T, H, _ = q.shape
    DV = v.shape[-1]
    C = min(GLA_CHUNK, T)
    n = -(-T // C)
    pad = n * C - T

    def to_chunks(a):
        a = jnp.pad(a.astype(jnp.float32), ((0, 0), (0, pad), (0, 0), (0, 0)))
        return a.reshape(B, n, C, H, a.shape[-1]).transpose(1, 0, 3, 2, 4)

    causal = jnp.tril(jnp.ones((C, C), dtype=bool))

    def step(S, inp):
        qc, kc, vc, gc = inp
        b = jnp.cumsum(gc, axis=2)
        b_last = b[:, :, -1:, :]
        o_inter = jnp.einsum('bhtd,bhde->bhte', qc * jnp.exp(b), S)
        diff = b[:, :, :, None, :] - b[:, :, None, :, :]
        decay = jnp.exp(jnp.where(causal[:, :, None], diff, -jnp.inf))
        att = jnp.einsum('bhtd,bhsd,bhtsd->bhts', qc, kc, decay)
        o_intra = jnp.einsum('bhts,bhse->bhte', att, vc)
        S = jnp.exp(b_last[:, :, 0, :, None]) * S + jnp.einsum('bhsd,bhse->bhde', kc * jnp.exp(b_last - b), vc)
        return S, o_inter + o_intra

    S, o = lax.scan(step, s0.astype(jnp.float32),
                    (to_chunks(q), to_chunks(k), to_chunks(v), to_chunks(log_a)))
    o = o.transpose(1, 0, 3, 2, 4).reshape(B, n * C, H, DV)[:, :T]
    return o, S


def gla_mixer(h, w_in, w_a2, b_a, g_out, w_out, s0):
    B, T, _ = h.shape
    HK, HV = GLA_HEADS * GLA_DK, GLA_HEADS * GLA_DV
    p = h @ w_in
    q, k, v, r, a_lr = jnp.split(p, [HK, 2 * HK, 2 * HK + HV, 2 * HK + 2 * HV], axis=-1)
    log_a = jax.nn.log_sigmoid((a_lr @ w_a2 + b_a).astype(jnp.float32)) / GLA_TAU
    q = q.reshape(B, T, GLA_HEADS, GLA_DK) * (GLA_DK ** -0.5)
    k = k.reshape(B, T, GLA_HEADS, GLA_DK)
    v = v.reshape(B, T, GLA_HEADS, GLA_DV)
    log_a = log_a.reshape(B, T, GLA_HEADS, GLA_DK)
    o, s_new = gla_scan(q, k, v, log_a, s0)
    o = o * lax.rsqrt(jnp.mean(o * o, axis=-1, keepdims=True) + EPS) * g_out.astype(jnp.float32)
    o = o.reshape(B, T, HV).astype(h.dtype) * jax.nn.silu(r)
    return o @ w_out, s_new.astype(s0.dtype)


def dsa_project(h, w_in, pos):
    B, T, _ = h.shape
    QW, KW, IW = ATT_HEADS * HEAD_DIM, KV_HEADS * HEAD_DIM, IDX_HEADS * IDX_DIM
    p = h @ w_in
    q, k, v, qi, ki, wi = jnp.split(
        p, [QW, QW + KW, QW + 2 * KW, QW + 2 * KW + IW, QW + 2 * KW + IW + IDX_DIM], axis=-1)
    q = rope_partial(q.reshape(B, T, ATT_HEADS, HEAD_DIM), pos)
    k = rope_partial(k.reshape(B, T, KV_HEADS, HEAD_DIM), pos)
    v = v.reshape(B, T, KV_HEADS, HEAD_DIM)
    qi = rope_partial(qi.reshape(B, T, IDX_HEADS, IDX_DIM), pos)
    ki = rope_partial(ki.reshape(B, T, 1, IDX_DIM), pos)[:, :, 0]
    wi = wi * (IDX_HEADS ** -0.5 * IDX_DIM ** -0.5)
    return q, k, v, qi, ki, wi


def indexer_scores(qi, wi, ki):
    dots = jnp.einsum('bthd,bsd->bths', qi.astype(jnp.float32), ki.astype(jnp.float32))
    return jnp.einsum('bth,bths->bts', wi.astype(jnp.float32), jax.nn.relu(dots))


def sparse_attend(q, kg, vg, valid):
    B, Tq = q.shape[:2]
    qg = q.reshape(B, Tq, KV_HEADS, ATT_HEADS // KV_HEADS, HEAD_DIM)
    s = jnp.einsum('btgrd,btkgd->btgrk', qg, kg).astype(jnp.float32) * (HEAD_DIM ** -0.5)
    s = jnp.where(valid[:, :, None, None, :], s, -jnp.inf)
    pr = jax.nn.softmax(s, axis=-1).astype(vg.dtype)
    o = jnp.einsum('btgrk,btkgd->btgrd', pr, vg)
    return o.reshape(B, Tq, ATT_HEADS * HEAD_DIM)


def gather_rows(a, idx):
    return jax.vmap(lambda aa, ii: aa[ii])(a, idx)


def dsa_prompt(h, w_in, w_out):
    B, T, _ = h.shape
    pos = jnp.arange(T)
    q, k, v, qi, ki, wi = dsa_project(h, w_in, pos)
    topk = min(TOPK_MAX, T // 4)
    nb = T // QBLOCK
    blk = lambda a: a.reshape((B, nb, QBLOCK) + a.shape[2:]).swapaxes(0, 1)
    key_pos = jnp.arange(T)

    def block(inp):
        qc, qic, wic, t0 = inp
        tpos = t0 + jnp.arange(QBLOCK)
        sc = indexer_scores(qic, wic, ki)
        sc = jnp.where(key_pos[None, None, :] <= tpos[None, :, None], sc, -jnp.inf)
        _, idx = lax.top_k(sc, topk)
        valid = idx <= tpos[None, :, None]
        return sparse_attend(qc, gather_rows(k, idx), gather_rows(v, idx), valid)

    o = lax.map(block, (blk(q), blk(qi), blk(wi), jnp.arange(nb) * QBLOCK))
    o = o.swapaxes(0, 1).reshape(B, T, ATT_HEADS * HEAD_DIM)
    return o @ w_out, k, v, ki


def dsa_sample(h, w_in, w_out, cache_k, cache_v, cache_ki, page_table):
    DB, Tn, _ = h.shape
    pos = PAST_LEN + jnp.arange(Tn)
    q, k, v, qi, ki, wi = dsa_project(h, w_in, pos)
    L = PAST_LEN + Tn
    topk = min(TOPK_MAX, L // 4)
    ki_past = cache_ki[page_table].reshape(DB, PAST_LEN, IDX_DIM)
    ki_all = jnp.concatenate([ki_past.astype(ki.dtype), ki], axis=1)
    sc = indexer_scores(qi, wi, ki_all)
    sc = jnp.where(jnp.arange(L)[None, None, :] <= pos[None, :, None], sc, -jnp.inf)
    _, idx = lax.top_k(sc, topk)
    valid = idx <= pos[None, :, None]
    in_past = idx < PAST_LEN
    pidx = jnp.minimum(idx, PAST_LEN - 1)
    phys = jax.vmap(lambda pt, ii: pt[ii])(page_table, pidx // PAGE_SIZE)
    slot = pidx % PAGE_SIZE
    nidx = jnp.clip(idx - PAST_LEN, 0, Tn - 1)
    sel = in_past[..., None, None]
    kg = jnp.where(sel, cache_k[phys, slot].astype(k.dtype), gather_rows(k, nidx))
    vg = jnp.where(sel, cache_v[phys, slot].astype(v.dtype), gather_rows(v, nidx))
    o = sparse_attend(q, kg, vg, valid)
    return o @ w_out, k, v, ki


def setup_inputs(seed: int = 0) -> dict:
    key = jax.random.key(seed)
    ks = iter(jax.random.split(key, 40))
    f32 = jnp.float32

    def w(shape, fan_in):
        return jax.random.normal(next(ks), shape, f32) * (fan_in ** -0.5)

    def gain(shape):
        return 1.0 + 0.02 * jax.random.normal(next(ks), shape, f32)

    n_pages = PAST_LEN // PAGE_SIZE
    n_used = DEC_BATCH * n_pages
    n_pool = n_used + max(1, n_used // 4)
    page_table = jax.random.permutation(next(ks), n_pool)[:n_used].reshape(DEC_BATCH, n_pages).astype(jnp.int32)

    x_prompt = jax.random.normal(next(ks), (BATCH, SEQ, D_MODEL), f32)
    x_sample = jax.random.normal(next(ks), (DEC_BATCH, DEC_SEQ, D_MODEL), f32)
    state_gla = jax.random.normal(next(ks), (N_GLA_LAYERS, DEC_BATCH, GLA_HEADS, GLA_DK, GLA_DV), f32)
    cache_k = jax.random.normal(next(ks), (N_DSA_LAYERS, n_pool, PAGE_SIZE, KV_HEADS, HEAD_DIM), f32)
    cache_v = jax.random.normal(next(ks), (N_DSA_LAYERS, n_pool, PAGE_SIZE, KV_HEADS, HEAD_DIM), f32)
    cache_idx_k = jax.random.normal(next(ks), (N_DSA_LAYERS, n_pool, PAGE_SIZE, IDX_DIM), f32)

    G, S = N_GLA_LAYERS, N_DSA_LAYERS
    return {
        "x_prompt": x_prompt,
        "x_sample": x_sample,
        "state_gla": state_gla,
        "cache_k": cache_k,
        "cache_v": cache_v,
        "cache_idx_k": cache_idx_k,
        "page_table": page_table,
        "gla_norm": gain((G, D_MODEL)),
        "gla_w_in": w((G, D_MODEL, GLA_IN), D_MODEL),
        "gla_w_a2": w((G, GLA_RANK, GLA_HEADS * GLA_DK), GLA_RANK),
        "gla_b_a": 0.1 * jax.random.normal(next(ks), (G, GLA_HEADS * GLA_DK), f32),
        "gla_out_norm": gain((G, GLA_HEADS, GLA_DV)),
        "gla_w_out": w((G, GLA_HEADS * GLA_DV, D_MODEL), GLA_HEADS * GLA_DV),
        "dense_norm": gain((G, D_MODEL)),
        "dense_w_gate": w((G, D_MODEL, D_FF), D_MODEL),
        "dense_w_up": w((G, D_MODEL, D_FF), D_MODEL),
        "dense_w_down": w((G, D_FF, D_MODEL), D_FF),
        "dsa_norm": gain((S, D_MODEL)),
        "dsa_w_in": w((S, D_MODEL, DSA_IN), D_MODEL),
        "dsa_w_out": w((S, ATT_HEADS * HEAD_DIM, D_MODEL), ATT_HEADS * HEAD_DIM),
        "moe_norm": gain((S, D_MODEL)),
        "moe_w_router": w((S, D_MODEL, N_EXPERTS), D_MODEL),
        "moe_w_gate": w((S, N_EXPERTS, D_MODEL, D_FF), D_MODEL),
        "moe_w_up": w((S, N_EXPERTS, D_MODEL, D_FF), D_MODEL),
        "moe_w_down": w((S, N_EXPERTS, D_FF, D_MODEL), D_FF),
        "final_norm": gain((D_MODEL,)),
    }


def reference(x_prompt, x_sample, state_gla, cache_k, cache_v, cache_idx_k, page_table,
              gla_norm, gla_w_in, gla_w_a2, gla_b_a, gla_out_norm, gla_w_out,
              dense_norm, dense_w_gate, dense_w_up, dense_w_down,
              dsa_norm, dsa_w_in, dsa_w_out,
              moe_norm, moe_w_router, moe_w_gate, moe_w_up, moe_w_down,
              final_norm):
    xp, xs = x_prompt, x_sample
    sg_p, sg_s = [], []
    kp, vp, ip, ks_, vs_, is_ = [], [], [], [], [], []
    for i in range(DEPTH):
        j = i // 2
        if i % 2 == 0:
            s0p = jnp.zeros((xp.shape[0], GLA_HEADS, GLA_DK, GLA_DV), state_gla.dtype)
            op, sp = gla_mixer(rmsnorm(xp, gla_norm[j]), gla_w_in[j], gla_w_a2[j], gla_b_a[j],
                               gla_out_norm[j], gla_w_out[j], s0p)
            os_, ss = gla_mixer(rmsnorm(xs, gla_norm[j]), gla_w_in[j], gla_w_a2[j], gla_b_a[j],
                                gla_out_norm[j], gla_w_out[j], state_gla[j])
            xp = xp + op
            xs = xs + os_
            sg_p.append(sp)
            sg_s.append(ss)
            xp = xp + swiglu(rmsnorm(xp, dense_norm[j]), dense_w_gate[j], dense_w_up[j], dense_w_down[j])
            xs = xs + swiglu(rmsnorm(xs, dense_norm[j]), dense_w_gate[j], dense_w_up[j], dense_w_down[j])
        else:
            op, kpn, vpn, ipn = dsa_prompt(rmsnorm(xp, dsa_norm[j]), dsa_w_in[j], dsa_w_out[j])
            os_, ksn, vsn, isn = dsa_sample(rmsnorm(xs, dsa_norm[j]), dsa_w_in[j], dsa_w_out[j],
                                            cache_k[j], cache_v[j], cache_idx_k[j], page_table)
            xp = xp + op
            xs = xs + os_
            kp.append(kpn); vp.append(vpn); ip.append(ipn)
            ks_.append(ksn); vs_.append(vsn); is_.append(isn)
            xp = xp + moe_ffn(rmsnorm(xp, moe_norm[j]), moe_w_router[j], moe_w_gate[j], moe_w_up[j], moe_w_down[j])
            xs = xs + moe_ffn(rmsnorm(xs, moe_norm[j]), moe_w_router[j], moe_w_gate[j], moe_w_up[j], moe_w_down[j])
    y_prompt = rmsnorm(xp, final_norm)
    y_sample = rmsnorm(xs, final_norm)
    return (y_prompt, y_sample, jnp.stack(sg_p), jnp.stack(sg_s),
            jnp.stack(kp), jnp.stack(vp), jnp.stack(ip),
            jnp.stack(ks_), jnp.stack(vs_), jnp.stack(is_))
```

```python
import functools

import jax, jax.numpy as jnp
from jax import lax
from jax.experimental import pallas as pl
from jax.experimental.pallas import tpu as pltpu

D_MODEL = 1024
BATCH = 8
SEQ = 4096
DEPTH = 2
DEC_BATCH = 32
DEC_SEQ = 1
PAST_LEN = 16384
PAGE_SIZE = 128
GLA_HEADS = 4
GLA_DK = D_MODEL // 2 // GLA_HEADS
GLA_DV = D_MODEL // GLA_HEADS
GLA_RANK = 16
GLA_TAU = 16.0
GLA_CHUNK = 32
ATT_HEADS = 16
KV_HEADS = 4
HEAD_DIM = D_MODEL // ATT_HEADS
IDX_HEADS = 8
IDX_DIM = 64
TOPK_MAX = 256
QBLOCK = 128
ROPE_THETA = 500000.0
ROPE_FRAC = 4
D_FF = 2816
N_EXPERTS = 8
TOP_K_EXPERTS = 2
EPS = 1e-6


def rmsnorm(x, g):
    xf = x.astype(jnp.float32)
    y = xf * lax.rsqrt(jnp.mean(xf * xf, axis=-1, keepdims=True) + EPS)
    return (y * g.astype(jnp.float32)).astype(x.dtype)


def rope_partial(x, pos):
    d_rot = x.shape[-1] // ROPE_FRAC
    half = d_rot // 2
    inv = ROPE_THETA ** (-jnp.arange(half, dtype=jnp.float32) / half)
    ang = pos.astype(jnp.float32)[:, None] * inv[None, :]
    cos = jnp.cos(ang)[:, None, :]
    sin = jnp.sin(ang)[:, None, :]
    xr = x[..., :d_rot].astype(jnp.float32)
    x1, x2 = xr[..., :half], xr[..., half:]
    rot = jnp.concatenate([x1 * cos - x2 * sin, x2 * cos + x1 * sin], axis=-1)
    return jnp.concatenate([rot.astype(x.dtype), x[..., d_rot:]], axis=-1)


def swiglu(h, w_gate, w_up, w_down):
    return (jax.nn.silu(h @ w_gate) * (h @ w_up)) @ w_down


def moe_ffn(h, w_router, w_gate, w_up, w_down):
    logits = (h @ w_router).astype(jnp.float32)
    top_val, top_idx = lax.top_k(logits, TOP_K_EXPERTS)
    gates = jax.nn.softmax(top_val, axis=-1)
    dense_gate = jnp.sum(jax.nn.one_hot(top_idx, N_EXPERTS, dtype=jnp.float32) * gates[..., None], axis=-2)
    dense_gate = dense_gate.astype(h.dtype)
    out = jnp.zeros_like(h)
    for e in range(N_EXPERTS):
        out = out + dense_gate[..., e:e + 1] * swiglu(h, w_gate[e], w_up[e], w_down[e])
    return out


def gla_scan(q, k, v, log_a, s0):
    B, T, H, _ = q.shape
    DV = v.shape[-1]
    C = min(GLA_CHUNK, T)
    n = -(-T // C)
    pad = n * C - T

    def to_chunks(a):
        a = jnp.pad(a.astype(jnp.float32), ((0, 0), (0, pad), (0, 0), (0, 0)))
        return a.reshape(B, n, C, H, a.shape[-1]).transpose(1, 0, 3, 2, 4)

    causal = jnp.tril(jnp.ones((C, C), dtype=bool))

    def step(S, inp):
        qc, kc, vc, gc = inp
        b = jnp.cumsum(gc, axis=2)
        b_last = b[:, :, -1:, :]
        o_inter = jnp.einsum('bhtd,bhde->bhte', qc * jnp.exp(b), S)
        diff = b[:, :, :, None, :] - b[:, :, None, :, :]
        decay = jnp.exp(jnp.where(causal[:, :, None], diff, -jnp.inf))
        att = jnp.einsum('bhtd,bhsd,bhtsd->bhts', qc, kc, decay)
        o_intra = jnp.einsum('bhts,bhse->bhte', att, vc)
        S = jnp.exp(b_last[:, :, 0, :, None]) * S + jnp.einsum('bhsd,bhse->bhde', kc * jnp.exp(b_last - b), vc)
        return S, o_inter + o_intra

    S, o = lax.scan(step, s0.astype(jnp.float32),
                    (to_chunks(q), to_chunks(k), to_chunks(v), to_chunks(log_a)))
    o = o.transpose(1, 0, 3, 2, 4).reshape(B, n * C, H, DV)[:, :T]
    return o, S


def gla_mixer(h, w_in, w_a2, b_a, g_out, w_out, s0):
    B, T, _ = h.shape
    HK, HV = GLA_HEADS * GLA_DK, GLA_HEADS * GLA_DV
    p = h @ w_in
    q, k, v, r, a_lr = jnp.split(p, [HK, 2 * HK, 2 * HK + HV, 2 * HK + 2 * HV], axis=-1)
    log_a = jax.nn.log_sigmoid((a_lr @ w_a2 + b_a).astype(jnp.float32)) / GLA_TAU
    q = q.reshape(B, T, GLA_HEADS, GLA_DK) * (GLA_DK ** -0.5)
    k = k.reshape(B, T, GLA_HEADS, GLA_DK)
    v = v.reshape(B, T, GLA_HEADS, GLA_DV)
    log_a = log_a.reshape(B, T, GLA_HEADS, GLA_DK)
    o, s_new = gla_scan(q, k, v, log_a, s0)
    o = o * lax.rsqrt(jnp.mean(o * o, axis=-1, keepdims=True) + EPS) * g_out.astype(jnp.float32)
    o = o.reshape(B, T, HV).astype(h.dtype) * jax.nn.silu(r)
    return o @ w_out, s_new.astype(s0.dtype)


def dsa_project(h, w_in, pos):
    B, T, _ = h.shape
    QW, KW, IW = ATT_HEADS * HEAD_DIM, KV_HEADS * HEAD_DIM, IDX_HEADS * IDX_DIM
    p = h @ w_in
    q, k, v, qi, ki, wi = jnp.split(
        p, [QW, QW + KW, QW + 2 * KW, QW + 2 * KW + IW, QW + 2 * KW + IW + IDX_DIM], axis=-1)
    q = rope_partial(q.reshape(B, T, ATT_HEADS, HEAD_DIM), pos)
    k = rope_partial(k.reshape(B, T, KV_HEADS, HEAD_DIM), pos)
    v = v.reshape(B, T, KV_HEADS, HEAD_DIM)
    qi = rope_partial(qi.reshape(B, T, IDX_HEADS, IDX_DIM), pos)
    ki = rope_partial(ki.reshape(B, T, 1, IDX_DIM), pos)[:, :, 0]
    wi = wi * (IDX_HEADS ** -0.5 * IDX_DIM ** -0.5)
    return q, k, v, qi, ki, wi


def indexer_scores(qi, wi, ki):
    dots = jnp.einsum('bthd,bsd->bths', qi.astype(jnp.float32), ki.astype(jnp.float32))
    return jnp.einsum('bth,bths->bts', wi.astype(jnp.float32), jax.nn.relu(dots))


def sparse_attend(q, kg, vg, valid):
    B, Tq = q.shape[:2]
    qg = q.reshape(B, Tq, KV_HEADS, ATT_HEADS // KV_HEADS, HEAD_DIM)
    s = jnp.einsum('btgrd,btkgd->btgrk', qg, kg).astype(jnp.float32) * (HEAD_DIM ** -0.5)
    s = jnp.where(valid[:, :, None, None, :], s, -jnp.inf)
    pr = jax.nn.softmax(s, axis=-1).astype(vg.dtype)
    o = jnp.einsum('btgrk,btkgd->btgrd', pr, vg)
    return o.reshape(B, Tq, ATT_HEADS * HEAD_DIM)


def gather_rows(a, idx):
    return jax.vmap(lambda aa, ii: aa[ii])(a, idx)


TQ = 128
CK = 256
MASK_BIAS = -1e30
INT32_MIN = -2 ** 31
INT32_MAX = 2 ** 31 - 1
KEY_NEG_INF = -2139095041
VMEM_LIMIT_BYTES = 48 * 1024 * 1024


def _dsa_attn_kernel(qi_ref, w_ref, q_ref, ki_ref, k_ref, vt_ref, o_ref,
                     key_sc, bias_sc, cut_sc, m_sc, l_sc, acc_sc, *, topk, n_idx_heads, n_groups, n_rep, hd):
    tq = w_ref.shape[1]
    i = pl.program_id(1)
    n_ch = (i * tq + tq + CK - 1) // CK
    t_idx = i * tq + lax.broadcasted_iota(jnp.int32, (CK, tq), 1)
    row_iota = lax.broadcasted_iota(jnp.int32, (CK, tq), 0)

    def score_chunk(c, carry):
        r0 = pl.multiple_of(c * CK, CK)
        d = jnp.dot(ki_ref[pl.ds(r0, CK), :], qi_ref[...], preferred_element_type=jnp.float32)
        acc = jnp.zeros((CK, tq), jnp.float32)
        for h in range(n_idx_heads):
            acc = acc + jnp.maximum(d[:, h * tq:(h + 1) * tq], 0.0) * w_ref[h:h + 1, :]
        acc = jnp.where(r0 + row_iota <= t_idx, acc, -jnp.inf)
        bits = pltpu.bitcast(acc, jnp.int32)
        key_sc[pl.ds(r0, CK), :] = jnp.where(bits < 0, bits ^ INT32_MAX, bits)
        return carry

    lax.fori_loop(0, n_ch, score_chunk, 0)

    def count_rows(pred):
        def body(c, acc):
            r0 = pl.multiple_of(c * CK, CK)
            hit = jnp.where(pred(key_sc[pl.ds(r0, CK), :], r0 + row_iota), 1, 0)
            return acc + jnp.sum(hit.reshape(CK // 8, 8, tq), axis=0)
        acc = lax.fori_loop(0, n_ch, body, jnp.zeros((8, tq), jnp.int32))
        return jnp.sum(acc, axis=0, keepdims=True)

    cnt_nonneg = count_rows(lambda key, rows: key >= 0)
    nonneg = cnt_nonneg >= topk
    thr0 = jnp.where(nonneg, 0, INT32_MIN)
    cnt0 = jnp.where(nonneg, cnt_nonneg, n_ch * CK)

    def bit_step(b, carry):
        thr, cnt_thr = carry
        cand = thr | jnp.left_shift(jnp.int32(1), 30 - b)
        cnt = count_rows(lambda key, rows: key >= cand)
        ok = cnt >= topk
        return jnp.where(ok, cand, thr), jnp.where(ok, cnt, cnt_thr)

    thr, cnt_thr = lax.fori_loop(0, 31, bit_step, (thr0, cnt0))

    excess = (cnt_thr > topk) & (thr > KEY_NEG_INF)
    cut_sc[...] = jnp.full((1, tq), INT32_MAX, jnp.int32)

    @pl.when(jnp.max(jnp.where(excess, 1, 0)) > 0)
    def _():
        need = topk - count_rows(lambda key, rows: key > thr)

        row_bits = (key_sc.shape[0] - 1).bit_length()

        def row_step(b, cut):
            cand = cut | jnp.left_shift(jnp.int32(1), row_bits - 1 - b)
            cnt = count_rows(lambda key, rows: (key == thr) & (rows < cand))
            return jnp.where(cnt < need, cand, cut)

        cut = lax.fori_loop(0, row_bits, row_step, jnp.zeros((1, tq), jnp.int32))
        cut_sc[...] = jnp.where(excess, cut, INT32_MAX)

    cut = cut_sc[...]

    def bias_chunk(c, carry):
        r0 = pl.multiple_of(c * CK, CK)
        key = key_sc[pl.ds(r0, CK), :]
        rows = r0 + row_iota
        sel = (rows <= t_idx) & ((key > thr) | ((key == thr) & (rows <= cut)))
        bias_sc[pl.ds(r0, CK), :] = jnp.where(sel, 0.0, MASK_BIAS)
        return carry

    lax.fori_loop(0, n_ch, bias_chunk, 0)

    n_heads = n_groups * n_rep
    m_sc[...] = jnp.full((n_heads, tq), MASK_BIAS, jnp.float32)
    l_sc[...] = jnp.zeros((n_heads, tq), jnp.float32)
    acc_sc[...] = jnp.zeros((n_heads * hd, tq), jnp.float32)

    def attn_chunk(c, carry):
        r0 = pl.multiple_of(c * CK, CK)
        bias = bias_sc[pl.ds(r0, CK), :]
        for g in range(n_groups):
            s = jnp.dot(k_ref[g, pl.ds(r0, CK), :], q_ref[g], preferred_element_type=jnp.float32)
            vt = vt_ref[g, c]
            for r in range(n_rep):
                hh = g * n_rep + r
                sr = s[:, r * tq:(r + 1) * tq] + bias
                m_old = m_sc[hh:hh + 1, :]
                m_new = jnp.maximum(m_old, jnp.max(sr, axis=0, keepdims=True))
                alpha = jnp.exp(m_old - m_new)
                p = jnp.exp(sr - m_new)
                l_sc[hh:hh + 1, :] = alpha * l_sc[hh:hh + 1, :] + jnp.sum(p, axis=0, keepdims=True)
                pv = jnp.dot(vt, p.astype(jnp.bfloat16), preferred_element_type=jnp.float32)
                acc_sc[hh * hd:(hh + 1) * hd, :] = alpha * acc_sc[hh * hd:(hh + 1) * hd, :] + pv
                m_sc[hh:hh + 1, :] = m_new
        return carry

    lax.fori_loop(0, n_ch, attn_chunk, 0)

    for hh in range(n_heads):
        o_ref[hh * hd:(hh + 1) * hd, :] = acc_sc[hh * hd:(hh + 1) * hd, :] / l_sc[hh:hh + 1, :]


def dsa_attention_prompt(q, k, v, qi, ki, wi, topk):
    B, T, H, hd = q.shape
    G = k.shape[2]
    R = H // G
    IH, ID = qi.shape[2], qi.shape[3]
    nb = T // TQ
    bf = jnp.bfloat16
    qi_l = qi.astype(bf).reshape(B, nb, TQ, IH, ID).transpose(0, 1, 4, 3, 2).reshape(B, nb, ID, IH * TQ)
    q_l = (q * (hd ** -0.5)).astype(bf).reshape(B, nb, TQ, G, R, hd).transpose(0, 1, 3, 5, 4, 2)
    q_l = q_l.reshape(B, nb, G, hd, R * TQ)
    w_l = wi.astype(jnp.float32).reshape(B, nb, TQ, IH).transpose(0, 1, 3, 2)
    k_l = k.astype(bf).transpose(0, 2, 1, 3)
    vt_l = v.astype(bf).reshape(B, T // CK, CK, G, hd).transpose(0, 3, 1, 4, 2)
    ki_l = ki.astype(bf)
    kern = functools.partial(_dsa_attn_kernel, topk=topk, n_idx_heads=IH, n_groups=G, n_rep=R, hd=hd)
    return pl.pallas_call(
        kern,
        grid=(B, nb),
        in_specs=[
            pl.BlockSpec((None, None, ID, IH * TQ), lambda b, i: (b, i, 0, 0)),
            pl.BlockSpec((None, None, IH, TQ), lambda b, i: (b, i, 0, 0)),
            pl.BlockSpec((None, None, G, hd, R * TQ), lambda b, i: (b, i, 0, 0, 0)),
            pl.BlockSpec((None, T, ID), lambda b, i: (b, 0, 0)),
            pl.BlockSpec((None, G, T, hd), lambda b, i: (b, 0, 0, 0)),
            pl.BlockSpec((None, G, T // CK, hd, CK), lambda b, i: (b, 0, 0, 0, 0)),
        ],
        out_specs=pl.BlockSpec((None, H * hd, TQ), lambda b, i: (b, 0, i)),
        out_shape=jax.ShapeDtypeStruct((B, H * hd, T), jnp.float32),
        scratch_shapes=[
            pltpu.VMEM((T, TQ), jnp.int32),
            pltpu.VMEM((T, TQ), jnp.float32),
            pltpu.VMEM((1, TQ), jnp.int32),
            pltpu.VMEM((H, TQ), jnp.float32),
            pltpu.VMEM((H, TQ), jnp.float32),
            pltpu.VMEM((H * hd, TQ), jnp.float32),
        ],
        compiler_params=pltpu.CompilerParams(
            dimension_semantics=("arbitrary", "arbitrary"), vmem_limit_bytes=VMEM_LIMIT_BYTES),
        name="dsa_attention_prompt",
    )(qi_l, w_l, q_l, ki_l, k_l, vt_l)


def dsa_prompt(h, w_in, w_out):
    B, T, _ = h.shape
    pos = jnp.arange(T)
    q, k, v, qi, ki, wi = dsa_project(h, w_in, pos)
    topk = min(TOPK_MAX, T // 4)
    ot = dsa_attention_prompt(q, k, v, qi, ki, wi, topk)
    return jnp.einsum('bct,cn->btn', ot, w_out), k, v, ki


def dsa_sample(h, w_in, w_out, cache_k, cache_v, cache_ki, page_table):
    DB, Tn, _ = h.shape
    pos = PAST_LEN + jnp.arange(Tn)
    q, k, v, qi, ki, wi = dsa_project(h, w_in, pos)
    L = PAST_LEN + Tn
    topk = min(TOPK_MAX, L // 4)
    ki_past = cache_ki[page_table].reshape(DB, PAST_LEN, IDX_DIM)
    ki_all = jnp.concatenate([ki_past.astype(ki.dtype), ki], axis=1)
    sc = indexer_scores(qi, wi, ki_all)
    sc = jnp.where(jnp.arange(L)[None, None, :] <= pos[None, :, None], sc, -jnp.inf)
    _, idx = lax.top_k(sc, topk)
    valid = idx <= pos[None, :, None]
    in_past = idx < PAST_LEN
    pidx = jnp.minimum(idx, PAST_LEN - 1)
    phys = jax.vmap(lambda pt, ii: pt[ii])(page_table, pidx // PAGE_SIZE)
    slot = pidx % PAGE_SIZE
    nidx = jnp.clip(idx - PAST_LEN, 0, Tn - 1)
    sel = in_past[..., None, None]
    kg = jnp.where(sel, cache_k[phys, slot].astype(k.dtype), gather_rows(k, nidx))
    vg = jnp.where(sel, cache_v[phys, slot].astype(v.dtype), gather_rows(v, nidx))
    o = sparse_attend(q, kg, vg, valid)
    return o @ w_out, k, v, ki


def _final_norm_kernel(x_ref, g_ref, o_ref):
    x = x_ref[...]
    o_ref[...] = x * lax.rsqrt(jnp.mean(x * x, axis=-1, keepdims=True) + EPS) * g_ref[...]


def final_rmsnorm(x, g):
    shp = x.shape
    x2 = x.reshape(-1, shp[-1])
    n = x2.shape[0]
    tm = min(n, 512)
    y = pl.pallas_call(
        _final_norm_kernel,
        grid=(n // tm,),
        in_specs=[pl.BlockSpec((tm, shp[-1]), lambda i: (i, 0)),
                  pl.BlockSpec((1, shp[-1]), lambda i: (0, 0))],
        out_specs=pl.BlockSpec((tm, shp[-1]), lambda i: (i, 0)),
        out_shape=jax.ShapeDtypeStruct(x2.shape, x2.dtype),
    )(x2, g.reshape(1, -1))
    return y.reshape(shp)


def kernel(x_prompt, x_sample, state_gla, cache_k, cache_v, cache_idx_k, page_table,
           gla_norm, gla_w_in, gla_w_a2, gla_b_a, gla_out_norm, gla_w_out,
           dense_norm, dense_w_gate, dense_w_up, dense_w_down,
           dsa_norm, dsa_w_in, dsa_w_out,
           moe_norm, moe_w_router, moe_w_gate, moe_w_up, moe_w_down,
           final_norm):
    xp, xs = x_prompt, x_sample
    sg_p, sg_s = [], []
    kp, vp, ip, ks_, vs_, is_ = [], [], [], [], [], []
    for i in range(DEPTH):
        j = i // 2
        if i % 2 == 0:
            s0p = jnp.zeros((xp.shape[0], GLA_HEADS, GLA_DK, GLA_DV), state_gla.dtype)
            op, sp = gla_mixer(rmsnorm(xp, gla_norm[j]), gla_w_in[j], gla_w_a2[j], gla_b_a[j],
                               gla_out_norm[j], gla_w_out[j], s0p)
            with jax.default_matmul_precision("highest"):
                os_, ss = gla_mixer(rmsnorm(xs, gla_norm[j]), gla_w_in[j], gla_w_a2[j], gla_b_a[j],
                                    gla_out_norm[j], gla_w_out[j], state_gla[j])
            xp = xp + op
            xs = xs + os_
            sg_p.append(sp)
            sg_s.append(ss)
            xp = xp + swiglu(rmsnorm(xp, dense_norm[j]), dense_w_gate[j], dense_w_up[j], dense_w_down[j])
            with jax.default_matmul_precision("highest"):
                xs = xs + swiglu(rmsnorm(xs, dense_norm[j]), dense_w_gate[j], dense_w_up[j], dense_w_down[j])
        else:
            op, kpn, vpn, ipn = dsa_prompt(rmsnorm(xp, dsa_norm[j]), dsa_w_in[j], dsa_w_out[j])
            with jax.default_matmul_precision("highest"):
                os_, ksn, vsn, isn = dsa_sample(rmsnorm(xs, dsa_norm[j]), dsa_w_in[j], dsa_w_out[j],
                                                cache_k[j], cache_v[j], cache_idx_k[j], page_table)
            xp = xp + op
            xs = xs + os_
            kp.append(kpn); vp.append(vpn); ip.append(ipn)
            ks_.append(ksn); vs_.append(vsn); is_.append(isn)
            xp = xp + moe_ffn(rmsnorm(xp, moe_norm[j]), moe_w_router[j], moe_w_gate[j], moe_w_up[j], moe_w_down[j])
            with jax.default_matmul_precision("highest"):
                xs = xs + moe_ffn(rmsnorm(xs, moe_norm[j]), moe_w_router[j], moe_w_gate[j], moe_w_up[j], moe_w_down[j])
    y_prompt = final_rmsnorm(xp, final_norm)
    y_sample = final_rmsnorm(xs, final_norm)
    return (y_prompt, y_sample, jnp.stack(sg_p), jnp.stack(sg_s),
            jnp.stack(kp), jnp.stack(vp), jnp.stack(ip),
            jnp.stack(ks_), jnp.stack(vs_), jnp.stack(is_))
```

```python
import functools

import jax, jax.numpy as jnp
from jax import lax
from jax.experimental import pallas as pl
from jax.experimental.pallas import tpu as pltpu

D_MODEL = 1024
PAST_LEN = 16384
PAGE_SIZE = 128
GLA_HEADS = 4
GLA_DK = D_MODEL // 2 // GLA_HEADS
GLA_DV = D_MODEL // GLA_HEADS
GLA_RANK = 16
GLA_TAU = 16.0
GLA_CHUNK = 32
ATT_HEADS = 16
KV_HEADS = 4
HEAD_DIM = D_MODEL // ATT_HEADS
IDX_HEADS = 8
IDX_DIM = 64
TOPK_MAX = 256
ROPE_THETA = 500000.0
ROPE_FRAC = 4
D_FF = 2816
N_EXPERTS = 8
EPS = 1e-6

LANES = 128
SUBLANES = 8
TQ = 128
CK = 256
MASK_BIAS = -1e30
INT32_MIN = -2 ** 31
INT32_MAX = 2 ** 31 - 1
KEY_NEG_INF = -2139095041
VMEM_LIMIT_BYTES = 52 * 1024 * 1024
F32 = jnp.float32
BF16 = jnp.bfloat16


def _round_up(n, m):
    return -(-n // m) * m


def _params(*sem):
    return pltpu.CompilerParams(dimension_semantics=sem, vmem_limit_bytes=VMEM_LIMIT_BYTES)


def _mxu_dot(a, b, hp):
    if hp:
        return jnp.dot(a.astype(F32), b.astype(F32), preferred_element_type=F32, precision=lax.Precision.HIGHEST)
    return jnp.dot(a.astype(BF16), b.astype(BF16), preferred_element_type=F32)


def _rms(x, g):
    return x * lax.rsqrt(jnp.mean(x * x, axis=-1, keepdims=True) + EPS) * g


def _norm_linear_kernel(*refs, apply_norm, has_res, has_tail, segs, hp):
    it = iter(refs)
    x_ref = next(it)
    g_ref = next(it) if apply_norm else None
    w_ref = next(it)
    res_ref = next(it) if has_res else None
    w2_ref, b2_ref = (next(it), next(it)) if has_tail else (None, None)
    out_refs = list(it)
    x = x_ref[...]
    if apply_norm:
        x = _rms(x.astype(F32), g_ref[...])
    y = _mxu_dot(x, w_ref[...], hp)
    if has_res:
        y = y + res_ref[...]
    for n, (o_ref, (off, width)) in enumerate(zip(out_refs, segs)):
        seg = y[:, off:off + width]
        if has_tail and n == len(segs) - 1:
            z = _mxu_dot(seg, w2_ref[...], hp) + b2_ref[...]
            seg = (jnp.minimum(z, 0.0) - jnp.log(1.0 + jnp.exp(-jnp.abs(z)))) * (1.0 / GLA_TAU)
        o_ref[...] = seg.astype(o_ref.dtype)


def norm_linear(x, g, w, *, res=None, segs=None, out_dtypes=None, tail=None, hp=False, tm=256, name="norm_linear"):
    M, K = x.shape
    N = w.shape[1]
    segs = segs or [(0, N)]
    out_dtypes = out_dtypes or [F32] * len(segs)
    tm = min(tm, M)
    args, in_specs = [x], [pl.BlockSpec((tm, K), lambda i: (i, 0))]
    if g is not None:
        args.append(g.reshape(1, K).astype(F32))
        in_specs.append(pl.BlockSpec((1, K), lambda i: (0, 0)))
    args.append(w)
    in_specs.append(pl.BlockSpec((K, N), lambda i: (0, 0)))
    if res is not None:
        args.append(res)
        in_specs.append(pl.BlockSpec((tm, N), lambda i: (i, 0)))
    out_widths = [wd for _, wd in segs]
    if tail is not None:
        w2, b2 = tail
        args += [w2, b2.reshape(1, -1).astype(F32)]
        in_specs += [pl.BlockSpec(w2.shape, lambda i: (0, 0)), pl.BlockSpec((1, w2.shape[1]), lambda i: (0, 0))]
        out_widths[-1] = w2.shape[1]
    kern = functools.partial(_norm_linear_kernel, apply_norm=g is not None, has_res=res is not None,
                             has_tail=tail is not None, segs=tuple(segs), hp=hp)
    outs = pl.pallas_call(
        kern,
        grid=(M // tm,),
        in_specs=in_specs,
        out_specs=[pl.BlockSpec((tm, wd), lambda i: (i, 0)) for wd in out_widths],
        out_shape=[jax.ShapeDtypeStruct((M, wd), dt) for wd, dt in zip(out_widths, out_dtypes)],
        compiler_params=_params("arbitrary"),
        name=name,
    )(*args)
    return outs


def _ffn_kernel(*refs, has_gate, hp):
    if has_gate:
        x_ref, g_ref, gate_ref, wg_ref, wu_ref, wd_ref, o_ref, h_sc = refs
    else:
        x_ref, g_ref, wg_ref, wu_ref, wd_ref, o_ref, h_sc = refs
        gate_ref = None
    e = pl.program_id(1)
    f = pl.program_id(2)

    @pl.when((e == 0) & (f == 0))
    def _():
        x = x_ref[...]
        h_sc[...] = _rms(x, g_ref[...]).astype(h_sc.dtype)
        o_ref[...] = x

    h = h_sc[...]
    a = _mxu_dot(h, wg_ref[...], hp)
    u = _mxu_dot(h, wu_ref[...], hp)
    y = _mxu_dot(a * jax.nn.sigmoid(a) * u, wd_ref[...], hp)
    if has_gate:
        gate = gate_ref[...]
        lane = lax.broadcasted_iota(jnp.int32, gate.shape, 1)
        y = y * jnp.sum(jnp.where(lane == e, gate, 0.0), axis=-1, keepdims=True)
    o_ref[...] += y


def ffn(x, g, w_gate, w_up, w_down, gates=None, *, hp=False, tm=512, tf=1408, name="ffn"):
    M, K = x.shape
    E, _, F = w_gate.shape
    tm = min(tm, M)
    args = [x, g.reshape(1, K).astype(F32)]
    in_specs = [pl.BlockSpec((tm, K), lambda i, e, f: (i, 0)), pl.BlockSpec((1, K), lambda i, e, f: (0, 0))]
    if gates is not None:
        args.append(gates)
        in_specs.append(pl.BlockSpec((tm, gates.shape[1]), lambda i, e, f: (i, 0)))
    args += [w_gate, w_up, w_down]
    in_specs += [pl.BlockSpec((None, K, tf), lambda i, e, f: (e, 0, f)),
                 pl.BlockSpec((None, K, tf), lambda i, e, f: (e, 0, f)),
                 pl.BlockSpec((None, tf, K), lambda i, e, f: (e, f, 0))]
    return pl.pallas_call(
        functools.partial(_ffn_kernel, has_gate=gates is not None, hp=hp),
        grid=(M // tm, E, F // tf),
        in_specs=in_specs,
        out_specs=pl.BlockSpec((tm, K), lambda i, e, f: (i, 0)),
        out_shape=jax.ShapeDtypeStruct((M, K), F32),
        scratch_shapes=[pltpu.VMEM((tm, K), F32 if hp else BF16)],
        compiler_params=_params("arbitrary", "arbitrary", "arbitrary"),
        name=name,
    )(*args)


def _router_kernel(x_ref, g_ref, w_ref, o_ref, *, n_experts):
    h = _rms(x_ref[...], g_ref[...])
    logits = _mxu_dot(h, w_ref[...], True)
    lane = lax.broadcasted_iota(jnp.int32, logits.shape, 1)
    logits = jnp.where(lane < n_experts, logits, -jnp.inf)
    m1 = jnp.max(logits, axis=-1, keepdims=True)
    i1 = jnp.min(jnp.where(logits == m1, lane, LANES), axis=-1, keepdims=True)
    rest = jnp.where(lane == i1, -jnp.inf, logits)
    m2 = jnp.max(rest, axis=-1, keepdims=True)
    i2 = jnp.min(jnp.where(rest == m2, lane, LANES), axis=-1, keepdims=True)
    e2 = jnp.exp(m2 - m1)
    denom = 1.0 + e2
    o_ref[...] = jnp.where(lane == i1, 1.0 / denom, 0.0) + jnp.where(lane == i2, e2 / denom, 0.0)


def router_gates(x, g, w_router, *, tm=512):
    M, K = x.shape
    E = w_router.shape[1]
    tm = min(tm, M)
    w = jnp.pad(w_router.astype(F32), ((0, 0), (0, LANES - E)))
    return pl.pallas_call(
        functools.partial(_router_kernel, n_experts=E),
        grid=(M // tm,),
        in_specs=[pl.BlockSpec((tm, K), lambda i: (i, 0)), pl.BlockSpec((1, K), lambda i: (0, 0)),
                  pl.BlockSpec((K, LANES), lambda i: (0, 0))],
        out_specs=pl.BlockSpec((tm, LANES), lambda i: (i, 0)),
        out_shape=jax.ShapeDtypeStruct((M, LANES), F32),
        compiler_params=_params("arbitrary"),
        name="moe_router",
    )(x, g.reshape(1, K).astype(F32), w)


def _gla_scan_kernel(q_ref, k_ref, g_ref, v_ref, r_ref, gout_ref, og_ref, s_ref, st_sc, *, chunk):
    t = pl.program_id(2)
    tt, dk = q_ref.shape
    n_tiles = chunk // SUBLANES

    @pl.when(t == 0)
    def _():
        st_sc[...] = jnp.zeros(st_sc.shape, F32)

    tri = (lax.broadcasted_iota(jnp.int32, (chunk, chunk), 0)
           >= lax.broadcasted_iota(jnp.int32, (chunk, chunk), 1)).astype(F32)
    row8 = lax.broadcasted_iota(jnp.int32, (SUBLANES, dk), 0)

    def chunk_body(c, carry):
        r0 = pl.multiple_of(c * chunk, chunk)
        qc = q_ref[pl.ds(r0, chunk), :] * (dk ** -0.5)
        kc = k_ref[pl.ds(r0, chunk), :]
        vc = v_ref[pl.ds(r0, chunk), :]
        b = jnp.dot(tri, g_ref[pl.ds(r0, chunk), :], preferred_element_type=F32,
                    precision=lax.Precision.HIGHEST)
        b_last = b[chunk - 1:chunk, :]
        st = st_sc[...]
        o_inter = lax.dot_general((qc * jnp.exp(b)).astype(BF16), st.astype(BF16),
                                  (((1,), (1,)), ((), ())), preferred_element_type=F32)
        o_t = [o_inter[i * SUBLANES:(i + 1) * SUBLANES, :] for i in range(n_tiles)]
        q_t = [qc[i * SUBLANES:(i + 1) * SUBLANES, :] for i in range(n_tiles)]
        b_t = [b[i * SUBLANES:(i + 1) * SUBLANES, :] for i in range(n_tiles)]
        for s in range(chunk):
            k_s, b_s, v_s = kc[s:s + 1, :], b[s:s + 1, :], vc[s:s + 1, :]
            for i in range(s // SUBLANES, n_tiles):
                diff = b_t[i] - b_s
                if i == s // SUBLANES:
                    diff = jnp.where(row8 + i * SUBLANES >= s, diff, -jnp.inf)
                att = jnp.sum(q_t[i] * k_s * jnp.exp(diff), axis=-1, keepdims=True)
                o_t[i] = o_t[i] + att * v_s
        o = jnp.concatenate(o_t, axis=0)
        kd = kc * jnp.exp(b_last - b)
        ds = lax.dot_general(vc.astype(BF16), kd.astype(BF16), (((0,), (0,)), ((), ())),
                             preferred_element_type=F32)
        st_sc[...] = jnp.exp(b_last) * st + ds
        on = _rms(o, gout_ref[...])
        rc = r_ref[pl.ds(r0, chunk), :]
        og_ref[pl.ds(r0, chunk), :] = (on * (rc * jax.nn.sigmoid(rc))).astype(og_ref.dtype)
        return carry

    lax.fori_loop(0, tt // chunk, chunk_body, 0)

    @pl.when(t == pl.num_programs(2) - 1)
    def _():
        s_ref[...] = st_sc[...].T


def gla_scan(q, k, g, v, r, g_out, *, tt=512):
    B, T, _ = q.shape
    H, DV = g_out.shape
    DK = q.shape[2] // H
    tt = min(tt, T)
    qk_spec = pl.BlockSpec((None, tt, DK), lambda b, h, t: (b, t, h))
    v_spec = pl.BlockSpec((None, tt, DV), lambda b, h, t: (b, t, h))
    return pl.pallas_call(
        functools.partial(_gla_scan_kernel, chunk=GLA_CHUNK),
        grid=(B, H, T // tt),
        in_specs=[qk_spec, qk_spec, qk_spec, v_spec, v_spec, pl.BlockSpec((None, 1, DV), lambda b, h, t: (h, 0, 0))],
        out_specs=[v_spec, pl.BlockSpec((None, None, DK, DV), lambda b, h, t: (b, h, 0, 0))],
        out_shape=[jax.ShapeDtypeStruct((B, T, H * DV), BF16), jax.ShapeDtypeStruct((B, H, DK, DV), F32)],
        scratch_shapes=[pltpu.VMEM((DV, DK), F32)],
        compiler_params=_params("arbitrary", "arbitrary", "arbitrary"),
        name="gla_scan",
    )(q, k, g, v, r, g_out.reshape(H, 1, DV).astype(F32))


def _gla_step_kernel(qa_ref, kcol_ref, acol_ref, qk_ref, v_ref, r_ref, gout_ref, s_ref, og_ref, sn_ref):
    s = s_ref[...]
    v = v_ref[...]
    o = jnp.sum(qa_ref[...] * s, axis=1, keepdims=True) + qk_ref[...] * v
    sn_ref[...] = acol_ref[...] * s + kcol_ref[...] * v
    on = _rms(o, gout_ref[...])
    r = r_ref[...]
    og_ref[...] = on * (r * jax.nn.sigmoid(r))


def gla_step(q, k, g, v, r, g_out, s0):
    B = q.shape[0]
    H, DV = g_out.shape
    DK = q.shape[1] // H
    qh = q.reshape(B, H, DK) * (DK ** -0.5)
    kh = k.reshape(B, H, DK)
    ah = jnp.exp(g.reshape(B, H, DK))
    col = lambda a: a[..., None]
    qk = jnp.sum(qh * kh, axis=-1)[..., None, None]
    row = lambda a: a.reshape(B, H, 1, DV)
    c_spec = pl.BlockSpec((None, H, DK, 1), lambda b: (b, 0, 0, 0))
    r_spec = pl.BlockSpec((None, H, 1, DV), lambda b: (b, 0, 0, 0))
    s_spec = pl.BlockSpec((None, H, DK, DV), lambda b: (b, 0, 0, 0))
    og, sn = pl.pallas_call(
        _gla_step_kernel,
        grid=(B,),
        in_specs=[c_spec, c_spec, c_spec, pl.BlockSpec((None, H, 1, 1), lambda b: (b, 0, 0, 0)),
                  r_spec, r_spec, pl.BlockSpec((H, 1, DV), lambda b: (0, 0, 0)), s_spec],
        out_specs=[r_spec, s_spec],
        out_shape=[jax.ShapeDtypeStruct((B, H, 1, DV), F32), jax.ShapeDtypeStruct((B, H, DK, DV), F32)],
        compiler_params=_params("arbitrary"),
        name="gla_step",
    )(col(qh * ah), col(kh), col(ah), qk, row(v), row(r), g_out.reshape(H, 1, DV).astype(F32), s0)
    return og.reshape(B, H * DV), sn


def _dsa_attn_kernel(qi_ref, w_ref, q_ref, ki_ref, k_ref, vt_ref, o_ref,
                     key_sc, bias_sc, cut_sc, m_sc, l_sc, acc_sc, *, topk, n_idx_heads, n_groups, n_rep, hd):
    tq = w_ref.shape[1]
    i = pl.program_id(1)
    n_ch = (i * tq + tq + CK - 1) // CK
    t_idx = i * tq + lax.broadcasted_iota(jnp.int32, (CK, tq), 1)
    row_iota = lax.broadcasted_iota(jnp.int32, (CK, tq), 0)

    def score_chunk(c, carry):
        r0 = pl.multiple_of(c * CK, CK)
        d = jnp.dot(ki_ref[pl.ds(r0, CK), :], qi_ref[...], preferred_element_type=F32)
        acc = jnp.zeros((CK, tq), F32)
        for h in range(n_idx_heads):
            acc = acc + jnp.maximum(d[:, h * tq:(h + 1) * tq], 0.0) * w_ref[h:h + 1, :]
        acc = jnp.where(r0 + row_iota <= t_idx, acc, -jnp.inf)
        bits = pltpu.bitcast(acc, jnp.int32)
        key_sc[pl.ds(r0, CK), :] = jnp.where(bits < 0, bits ^ INT32_MAX, bits)
        return carry

    lax.fori_loop(0, n_ch, score_chunk, 0)

    def count_rows(pred):
        def body(c, acc):
            r0 = pl.multiple_of(c * CK, CK)
            hit = jnp.where(pred(key_sc[pl.ds(r0, CK), :], r0 + row_iota), 1, 0)
            return acc + jnp.sum(hit.reshape(CK // SUBLANES, SUBLANES, tq), axis=0)
        acc = lax.fori_loop(0, n_ch, body, jnp.zeros((SUBLANES, tq), jnp.int32))
        return jnp.sum(acc, axis=0, keepdims=True)

    cnt_nonneg = count_rows(lambda key, rows: key >= 0)
    nonneg = cnt_nonneg >= topk
    thr0 = jnp.where(nonneg, 0, INT32_MIN)
    cnt0 = jnp.where(nonneg, cnt_nonneg, n_ch * CK)

    def bit_step(b, carry):
        thr, cnt_thr = carry
        cand = thr | jnp.left_shift(jnp.int32(1), 30 - b)
        cnt = count_rows(lambda key, rows: key >= cand)
        ok = cnt >= topk
        return jnp.where(ok, cand, thr), jnp.where(ok, cnt, cnt_thr)

    thr, cnt_thr = lax.fori_loop(0, 31, bit_step, (thr0, cnt0))

    excess = (cnt_thr > topk) & (thr > KEY_NEG_INF)
    cut_sc[...] = jnp.full((1, tq), INT32_MAX, jnp.int32)

    @pl.when(jnp.max(jnp.where(excess, 1, 0)) > 0)
    def _():
        need = topk - count_rows(lambda key, rows: key > thr)
        row_bits = (key_sc.shape[0] - 1).bit_length()

        def row_step(b, cut):
            cand = cut | jnp.left_shift(jnp.int32(1), row_bits - 1 - b)
            cnt = count_rows(lambda key, rows: (key == thr) & (rows < cand))
            return jnp.where(cnt < need, cand, cut)

        cut = lax.fori_loop(0, row_bits, row_step, jnp.zeros((1, tq), jnp.int32))
        cut_sc[...] = jnp.where(excess, cut, INT32_MAX)

    cut = cut_sc[...]

    def bias_chunk(c, carry):
        r0 = pl.multiple_of(c * CK, CK)
        key = key_sc[pl.ds(r0, CK), :]
        rows = r0 + row_iota
        sel = (rows <= t_idx) & ((key > thr) | ((key == thr) & (rows <= cut)))
        bias_sc[pl.ds(r0, CK), :] = jnp.where(sel, 0.0, MASK_BIAS)
        return carry

    lax.fori_loop(0, n_ch, bias_chunk, 0)

    n_heads = n_groups * n_rep
    m_sc[...] = jnp.full((n_heads, tq), MASK_BIAS, F32)
    l_sc[...] = jnp.zeros((n_heads, tq), F32)
    acc_sc[...] = jnp.zeros((n_heads * hd, tq), F32)

    def attn_chunk(c, carry):
        r0 = pl.multiple_of(c * CK, CK)
        bias = bias_sc[pl.ds(r0, CK), :]
        for g in range(n_groups):
            s = jnp.dot(k_ref[g, pl.ds(r0, CK), :], q_ref[g], preferred_element_type=F32)
            vt = vt_ref[g, c]
            for r in range(n_rep):
                hh = g * n_rep + r
                sr = s[:, r * tq:(r + 1) * tq] + bias
                m_old = m_sc[hh:hh + 1, :]
                m_new = jnp.maximum(m_old, jnp.max(sr, axis=0, keepdims=True))
                alpha = jnp.exp(m_old - m_new)
                p = jnp.exp(sr - m_new)
                l_sc[hh:hh + 1, :] = alpha * l_sc[hh:hh + 1, :] + jnp.sum(p, axis=0, keepdims=True)
                pv = jnp.dot(vt, p.astype(BF16), preferred_element_type=F32)
                acc_sc[hh * hd:(hh + 1) * hd, :] = alpha * acc_sc[hh * hd:(hh + 1) * hd, :] + pv
                m_sc[hh:hh + 1, :] = m_new
        return carry

    lax.fori_loop(0, n_ch, attn_chunk, 0)

    for hh in range(n_heads):
        acc_sc[hh * hd:(hh + 1) * hd, :] = acc_sc[hh * hd:(hh + 1) * hd, :] / l_sc[hh:hh + 1, :]
    o_ref[...] = acc_sc[...].T.astype(o_ref.dtype)


def dsa_attention_prompt(q, k, v, qi, ki, wi, topk):
    B, T, H, hd = q.shape
    G = k.shape[2]
    R = H // G
    IH, ID = qi.shape[2], qi.shape[3]
    nb = T // TQ
    qi_l = qi.astype(BF16).reshape(B, nb, TQ, IH, ID).transpose(0, 1, 4, 3, 2).reshape(B, nb, ID, IH * TQ)
    q_l = (q * (hd ** -0.5)).astype(BF16).reshape(B, nb, TQ, G, R, hd).transpose(0, 1, 3, 5, 4, 2)
    q_l = q_l.reshape(B, nb, G, hd, R * TQ)
    w_l = wi.astype(F32).reshape(B, nb, TQ, IH).transpose(0, 1, 3, 2)
    k_l = k.astype(BF16).transpose(0, 2, 1, 3)
    vt_l = v.astype(BF16).reshape(B, T // CK, CK, G, hd).transpose(0, 3, 1, 4, 2)
    ki_l = ki.astype(BF16)
    kern = functools.partial(_dsa_attn_kernel, topk=topk, n_idx_heads=IH, n_groups=G, n_rep=R, hd=hd)
    return pl.pallas_call(
        kern,
        grid=(B, nb),
        in_specs=[
            pl.BlockSpec((None, None, ID, IH * TQ), lambda b, i: (b, i, 0, 0)),
            pl.BlockSpec((None, None, IH, TQ), lambda b, i: (b, i, 0, 0)),
            pl.BlockSpec((None, None, G, hd, R * TQ), lambda b, i: (b, i, 0, 0, 0)),
            pl.BlockSpec((None, T, ID), lambda b, i: (b, 0, 0)),
            pl.BlockSpec((None, G, T, hd), lambda b, i: (b, 0, 0, 0)),
            pl.BlockSpec((None, G, T // CK, hd, CK), lambda b, i: (b, 0, 0, 0, 0)),
        ],
        out_specs=pl.BlockSpec((None, TQ, H * hd), lambda b, i: (b, i, 0)),
        out_shape=jax.ShapeDtypeStruct((B, T, H * hd), BF16),
        scratch_shapes=[
            pltpu.VMEM((T, TQ), jnp.int32),
            pltpu.VMEM((T, TQ), F32),
            pltpu.VMEM((1, TQ), jnp.int32),
            pltpu.VMEM((H, TQ), F32),
            pltpu.VMEM((H, TQ), F32),
            pltpu.VMEM((H * hd, TQ), F32),
        ],
        compiler_params=_params("arbitrary", "arbitrary"),
        name="dsa_attention_prompt",
    )(qi_l, w_l, q_l, ki_l, k_l, vt_l)


def _final_norm_kernel(x_ref, g_ref, o_ref):
    o_ref[...] = _rms(x_ref[...], g_ref[...])


def final_rmsnorm(x, g, *, tm=512):
    M, K = x.shape
    tm = min(tm, M)
    return pl.pallas_call(
        _final_norm_kernel,
        grid=(M // tm,),
        in_specs=[pl.BlockSpec((tm, K), lambda i: (i, 0)), pl.BlockSpec((1, K), lambda i: (0, 0))],
        out_specs=pl.BlockSpec((tm, K), lambda i: (i, 0)),
        out_shape=jax.ShapeDtypeStruct((M, K), F32),
        compiler_params=_params("arbitrary"),
        name="final_rmsnorm",
    )(x, g.reshape(1, K).astype(F32))


def rope_partial(x, pos):
    d_rot = x.shape[-1] // ROPE_FRAC
    half = d_rot // 2
    inv = ROPE_THETA ** (-jnp.arange(half, dtype=F32) / half)
    ang = pos.astype(F32)[:, None] * inv[None, :]
    cos = jnp.cos(ang)[:, None, :]
    sin = jnp.sin(ang)[:, None, :]
    xr = x[..., :d_rot].astype(F32)
    x1, x2 = xr[..., :half], xr[..., half:]
    rot = jnp.concatenate([x1 * cos - x2 * sin, x2 * cos + x1 * sin], axis=-1)
    return jnp.concatenate([rot.astype(x.dtype), x[..., d_rot:]], axis=-1)


def _pad_cols(w, n):
    return jnp.pad(w, ((0, 0), (0, n - w.shape[1])))


def gla_layer(x, B, T, norm, w_in, w_a2, b_a, g_out, w_out, s0, hp):
    wdt = F32 if hp else BF16
    HK, HV = GLA_HEADS * GLA_DK, GLA_HEADS * GLA_DV
    n_pad = _round_up(w_in.shape[1], LANES)
    segs = [(0, HK), (HK, HK), (2 * HK, HV), (2 * HK + HV, HV), (2 * HK + 2 * HV, n_pad - 2 * HK - 2 * HV)]
    w2 = jnp.pad(w_a2, ((0, segs[-1][1] - GLA_RANK), (0, 0))).astype(wdt)
    q, k, v, r, g = norm_linear(x, norm, _pad_cols(w_in, n_pad).astype(wdt), segs=segs, tail=(w2, b_a),
                                hp=hp, name="gla_project")
    if s0 is None:
        sh = lambda a: a.reshape(B, T, a.shape[1])
        og, s_new = gla_scan(sh(q), sh(k), sh(g), sh(v), sh(r), g_out)
        og = og.reshape(B * T, HV)
    else:
        og, s_new = gla_step(q, k, g, v, r, g_out, s0)
    (x,) = norm_linear(og, None, w_out.astype(wdt), res=x, hp=hp, tm=512, name="gla_out_project")
    return x, s_new


def dsa_project(x, B, T, norm, w_in, pos, hp):
    wdt = F32 if hp else BF16
    QW, KW, IW = ATT_HEADS * HEAD_DIM, KV_HEADS * HEAD_DIM, IDX_HEADS * IDX_DIM
    n_pad = _round_up(w_in.shape[1], LANES)
    off_i = QW + 2 * KW + IW
    segs = [(0, QW), (QW, KW), (QW + KW, KW), (QW + 2 * KW, IW), (off_i, n_pad - off_i)]
    q, k, v, qi, kw = norm_linear(x, norm, _pad_cols(w_in, n_pad).astype(wdt), segs=segs, hp=hp, name="dsa_project")
    ki, wi = kw[:, :IDX_DIM], kw[:, IDX_DIM:IDX_DIM + IDX_HEADS]
    q = rope_partial(q.reshape(B, T, ATT_HEADS, HEAD_DIM), pos)
    k = rope_partial(k.reshape(B, T, KV_HEADS, HEAD_DIM), pos)
    v = v.reshape(B, T, KV_HEADS, HEAD_DIM)
    qi = rope_partial(qi.reshape(B, T, IDX_HEADS, IDX_DIM), pos)
    ki = rope_partial(ki.reshape(B, T, 1, IDX_DIM), pos)[:, :, 0]
    wi = wi.reshape(B, T, IDX_HEADS) * (IDX_HEADS ** -0.5 * IDX_DIM ** -0.5)
    return q, k, v, qi, ki, wi


def dsa_prompt_layer(x, B, T, norm, w_in, w_out):
    q, k, v, qi, ki, wi = dsa_project(x, B, T, norm, w_in, jnp.arange(T), False)
    o = dsa_attention_prompt(q, k, v, qi, ki, wi, min(TOPK_MAX, T // 4))
    (x,) = norm_linear(o.reshape(B * T, -1), None, w_out.astype(BF16), res=x, tm=512, name="dsa_out_project")
    return x, k, v, ki


def dsa_sample_layer(x, norm, w_in, w_out, cache_k, cache_v, cache_ki, page_table):
    DB, Tn = x.shape[0], 1
    pos = PAST_LEN + jnp.arange(Tn)
    q, k, v, qi, ki, wi = dsa_project(x, DB, Tn, norm, w_in, pos, True)
    L = PAST_LEN + Tn
    topk = min(TOPK_MAX, L // 4)
    with jax.default_matmul_precision("highest"):
        ki_past = cache_ki[page_table].reshape(DB, PAST_LEN, IDX_DIM)
        ki_all = jnp.concatenate([ki_past, ki], axis=1)
        dots = jnp.einsum('bthd,bsd->bths', qi, ki_all)
        sc = jnp.einsum('bth,bths->bts', wi, jax.nn.relu(dots))
        sc = jnp.where(jnp.arange(L)[None, None, :] <= pos[None, :, None], sc, -jnp.inf)
        _, idx = lax.top_k(sc, topk)
        valid = idx <= pos[None, :, None]
        in_past = idx < PAST_LEN
        pidx = jnp.minimum(idx, PAST_LEN - 1)
        phys = jax.vmap(lambda pt, ii: pt[ii])(page_table, pidx // PAGE_SIZE)
        slot = pidx % PAGE_SIZE
        nidx = jnp.clip(idx - PAST_LEN, 0, Tn - 1)
        sel = in_past[..., None, None]
        gather_rows = jax.vmap(lambda aa, ii: aa[ii])
        kg = jnp.where(sel, cache_k[phys, slot], gather_rows(k, nidx))
        vg = jnp.where(sel, cache_v[phys, slot], gather_rows(v, nidx))
        qg = q.reshape(DB, Tn, KV_HEADS, ATT_HEADS // KV_HEADS, HEAD_DIM)
        s = jnp.einsum('btgrd,btkgd->btgrk', qg, kg) * (HEAD_DIM ** -0.5)
        s = jnp.where(valid[:, :, None, None, :], s, -jnp.inf)
        pr = jax.nn.softmax(s, axis=-1)
        o = jnp.einsum('btgrk,btkgd->btgrd', pr, vg).reshape(DB * Tn, ATT_HEADS * HEAD_DIM)
    (x,) = norm_linear(o, None, w_out, res=x, hp=True, name="dsa_out_project_s")
    return x, k, v, ki


def kernel(x_prompt, x_sample, state_gla, cache_k, cache_v, cache_idx_k, page_table,
           gla_norm, gla_w_in, gla_w_a2, gla_b_a, gla_out_norm, gla_w_out,
           dense_norm, dense_w_gate, dense_w_up, dense_w_down,
           dsa_norm, dsa_w_in, dsa_w_out,
           moe_norm, moe_w_router, moe_w_gate, moe_w_up, moe_w_down,
           final_norm):
    B, T, D = x_prompt.shape
    DB = x_sample.shape[0]
    xp = x_prompt.reshape(B * T, D)
    xs = x_sample.reshape(DB, D)
    sg_p, sg_s = [], []
    kp, vp, ip, ks_, vs_, is_ = [], [], [], [], [], []
    for i in range(gla_norm.shape[0] + dsa_norm.shape[0]):
        j = i // 2
        if i % 2 == 0:
            xp, sp = gla_layer(xp, B, T, gla_norm[j], gla_w_in[j], gla_w_a2[j], gla_b_a[j], gla_out_norm[j],
                               gla_w_out[j], None, False)
            xs, ss = gla_layer(xs, DB, 1, gla_norm[j], gla_w_in[j], gla_w_a2[j], gla_b_a[j], gla_out_norm[j],
                               gla_w_out[j], state_gla[j], True)
            sg_p.append(sp)
            sg_s.append(ss)
            wg, wu, wd = dense_w_gate[j][None], dense_w_up[j][None], dense_w_down[j][None]
            xp = ffn(xp, dense_norm[j], wg.astype(BF16), wu.astype(BF16), wd.astype(BF16), name="dense_ffn")
            xs = ffn(xs, dense_norm[j], wg, wu, wd, hp=True, name="dense_ffn_s")
        else:
            xp, kpn, vpn, ipn = dsa_prompt_layer(xp, B, T, dsa_norm[j], dsa_w_in[j], dsa_w_out[j])
            xs, ksn, vsn, isn = dsa_sample_layer(xs, dsa_norm[j], dsa_w_in[j], dsa_w_out[j],
                                                 cache_k[j], cache_v[j], cache_idx_k[j], page_table)
            kp.append(kpn); vp.append(vpn); ip.append(ipn)
            ks_.append(ksn); vs_.append(vsn); is_.append(isn)
            wg, wu, wd = moe_w_gate[j], moe_w_up[j], moe_w_down[j]
            xp = ffn(xp, moe_norm[j], wg.astype(BF16), wu.astype(BF16), wd.astype(BF16),
                     router_gates(xp, moe_norm[j], moe_w_router[j]), name="moe_ffn")
            xs = ffn(xs, moe_norm[j], wg, wu, wd, router_gates(xs, moe_norm[j], moe_w_router[j]),
                     hp=True, name="moe_ffn_s")
    y_prompt = final_rmsnorm(xp, final_norm).reshape(B, T, D)
    y_sample = final_rmsnorm(xs, final_norm).reshape(DB, 1, D)
    return (y_prompt, y_sample, jnp.stack(sg_p), jnp.stack(sg_s),
            jnp.stack(kp), jnp.stack(vp), jnp.stack(ip),
            jnp.stack(ks_), jnp.stack(vs_), jnp.stack(is_))
```

```python
import functools

import jax, jax.numpy as jnp
from jax import lax
from jax.experimental import pallas as pl
from jax.experimental.pallas import tpu as pltpu

D_MODEL = 1024
PAST_LEN = 16384
PAGE_SIZE = 128
GLA_HEADS = 4
GLA_DK = D_MODEL // 2 // GLA_HEADS
GLA_DV = D_MODEL // GLA_HEADS
GLA_RANK = 16
GLA_TAU = 16.0
GLA_CHUNK = 32
ATT_HEADS = 16
KV_HEADS = 4
HEAD_DIM = D_MODEL // ATT_HEADS
IDX_HEADS = 8
IDX_DIM = 64
TOPK_MAX = 256
ROPE_THETA = 500000.0
ROPE_FRAC = 4
D_FF = 2816
N_EXPERTS = 8
EPS = 1e-6

LANES = 128
SUBLANES = 8
TQ = 128
CK = 256
MASK_BIAS = -1e30
INT32_MIN = -2 ** 31
INT32_MAX = 2 ** 31 - 1
KEY_NEG_INF = -2139095041
VMEM_LIMIT_BYTES = 52 * 1024 * 1024
F32 = jnp.float32
BF16 = jnp.bfloat16


def _round_up(n, m):
    return -(-n // m) * m


def _params(*sem):
    return pltpu.CompilerParams(dimension_semantics=sem, vmem_limit_bytes=VMEM_LIMIT_BYTES)


def _mxu_dot(a, b, hp):
    if hp:
        return jnp.dot(a.astype(F32), b.astype(F32), preferred_element_type=F32, precision=lax.Precision.HIGHEST)
    return jnp.dot(a.astype(BF16), b.astype(BF16), preferred_element_type=F32)


def _rms(x, g):
    return x * lax.rsqrt(jnp.mean(x * x, axis=-1, keepdims=True) + EPS) * g


def _norm_linear_kernel(*refs, apply_norm, has_res, has_tail, segs, hp):
    it = iter(refs)
    x_ref = next(it)
    g_ref = next(it) if apply_norm else None
    w_ref = next(it)
    res_ref = next(it) if has_res else None
    w2_ref, b2_ref = (next(it), next(it)) if has_tail else (None, None)
    out_refs = list(it)
    x = x_ref[...]
    if apply_norm:
        x = _rms(x.astype(F32), g_ref[...])
    y = _mxu_dot(x, w_ref[...], hp)
    if has_res:
        y = y + res_ref[...]
    for n, (o_ref, (off, width)) in enumerate(zip(out_refs, segs)):
        seg = y[:, off:off + width]
        if has_tail and n == len(segs) - 1:
            z = _mxu_dot(seg, w2_ref[...], hp) + b2_ref[...]
            seg = (jnp.minimum(z, 0.0) - jnp.log(1.0 + jnp.exp(-jnp.abs(z)))) * (1.0 / GLA_TAU)
        o_ref[...] = seg.astype(o_ref.dtype)


def norm_linear(x, g, w, *, res=None, segs=None, out_dtypes=None, tail=None, hp=False, tm=256, name="norm_linear"):
    M, K = x.shape
    N = w.shape[1]
    segs = segs or [(0, N)]
    out_dtypes = out_dtypes or [F32] * len(segs)
    tm = min(tm, M)
    args, in_specs = [x], [pl.BlockSpec((tm, K), lambda i: (i, 0))]
    if g is not None:
        args.append(g.reshape(1, K).astype(F32))
        in_specs.append(pl.BlockSpec((1, K), lambda i: (0, 0)))
    args.append(w)
    in_specs.append(pl.BlockSpec((K, N), lambda i: (0, 0)))
    if res is not None:
        args.append(res)
        in_specs.append(pl.BlockSpec((tm, N), lambda i: (i, 0)))
    out_widths = [wd for _, wd in segs]
    if tail is not None:
        w2, b2 = tail
        args += [w2, b2.reshape(1, -1).astype(F32)]
        in_specs += [pl.BlockSpec(w2.shape, lambda i: (0, 0)), pl.BlockSpec((1, w2.shape[1]), lambda i: (0, 0))]
        out_widths[-1] = w2.shape[1]
    kern = functools.partial(_norm_linear_kernel, apply_norm=g is not None, has_res=res is not None,
                             has_tail=tail is not None, segs=tuple(segs), hp=hp)
    outs = pl.pallas_call(
        kern,
        grid=(M // tm,),
        in_specs=in_specs,
        out_specs=[pl.BlockSpec((tm, wd), lambda i: (i, 0)) for wd in out_widths],
        out_shape=[jax.ShapeDtypeStruct((M, wd), dt) for wd, dt in zip(out_widths, out_dtypes)],
        compiler_params=_params("arbitrary"),
        name=name,
    )(*args)
    return outs


def _ffn_kernel(*refs, has_gate, hp):
    if has_gate:
        x_ref, g_ref, gate_ref, wg_ref, wu_ref, wd_ref, o_ref, h_sc = refs
    else:
        x_ref, g_ref, wg_ref, wu_ref, wd_ref, o_ref, h_sc = refs
        gate_ref = None
    e = pl.program_id(1)
    f = pl.program_id(2)

    @pl.when((e == 0) & (f == 0))
    def _():
        x = x_ref[...]
        h_sc[...] = _rms(x, g_ref[...]).astype(h_sc.dtype)
        o_ref[...] = x

    h = h_sc[...]
    a = _mxu_dot(h, wg_ref[...], hp)
    u = _mxu_dot(h, wu_ref[...], hp)
    y = _mxu_dot(a * jax.nn.sigmoid(a) * u, wd_ref[...], hp)
    if has_gate:
        gate = gate_ref[...]
        lane = lax.broadcasted_iota(jnp.int32, gate.shape, 1)
        y = y * jnp.sum(jnp.where(lane == e, gate, 0.0), axis=-1, keepdims=True)
    o_ref[...] += y


def ffn(x, g, w_gate, w_up, w_down, gates=None, *, hp=False, tm=512, tf=1408, name="ffn"):
    M, K = x.shape
    E, _, F = w_gate.shape
    tm = min(tm, M)
    args = [x, g.reshape(1, K).astype(F32)]
    in_specs = [pl.BlockSpec((tm, K), lambda i, e, f: (i, 0)), pl.BlockSpec((1, K), lambda i, e, f: (0, 0))]
    if gates is not None:
        args.append(gates)
        in_specs.append(pl.BlockSpec((tm, gates.shape[1]), lambda i, e, f: (i, 0)))
    args += [w_gate, w_up, w_down]
    in_specs += [pl.BlockSpec((None, K, tf), lambda i, e, f: (e, 0, f)),
                 pl.BlockSpec((None, K, tf), lambda i, e, f: (e, 0, f)),
                 pl.BlockSpec((None, tf, K), lambda i, e, f: (e, f, 0))]
    return pl.pallas_call(
        functools.partial(_ffn_kernel, has_gate=gates is not None, hp=hp),
        grid=(M // tm, E, F // tf),
        in_specs=in_specs,
        out_specs=pl.BlockSpec((tm, K), lambda i, e, f: (i, 0)),
        out_shape=jax.ShapeDtypeStruct((M, K), F32),
        scratch_shapes=[pltpu.VMEM((tm, K), F32 if hp else BF16)],
        compiler_params=_params("arbitrary", "arbitrary", "arbitrary"),
        name=name,
    )(*args)


def _router_kernel(x_ref, g_ref, w_ref, o_ref, *, n_experts):
    h = _rms(x_ref[...], g_ref[...])
    logits = _mxu_dot(h, w_ref[...], True)
    lane = lax.broadcasted_iota(jnp.int32, logits.shape, 1)
    logits = jnp.where(lane < n_experts, logits, -jnp.inf)
    m1 = jnp.max(logits, axis=-1, keepdims=True)
    i1 = jnp.min(jnp.where(logits == m1, lane, LANES), axis=-1, keepdims=True)
    rest = jnp.where(lane == i1, -jnp.inf, logits)
    m2 = jnp.max(rest, axis=-1, keepdims=True)
    i2 = jnp.min(jnp.where(rest == m2, lane, LANES), axis=-1, keepdims=True)
    e2 = jnp.exp(m2 - m1)
    denom = 1.0 + e2
    o_ref[...] = jnp.where(lane == i1, 1.0 / denom, 0.0) + jnp.where(lane == i2, e2 / denom, 0.0)


def router_gates(x, g, w_router, *, tm=512):
    M, K = x.shape
    E = w_router.shape[1]
    tm = min(tm, M)
    w = jnp.pad(w_router.astype(F32), ((0, 0), (0, LANES - E)))
    return pl.pallas_call(
        functools.partial(_router_kernel, n_experts=E),
        grid=(M // tm,),
        in_specs=[pl.BlockSpec((tm, K), lambda i: (i, 0)), pl.BlockSpec((1, K), lambda i: (0, 0)),
                  pl.BlockSpec((K, LANES), lambda i: (0, 0))],
        out_specs=pl.BlockSpec((tm, LANES), lambda i: (i, 0)),
        out_shape=jax.ShapeDtypeStruct((M, LANES), F32),
        compiler_params=_params("arbitrary"),
        name="moe_router",
    )(x, g.reshape(1, K).astype(F32), w)


MOE_TB = 1024
MOE_SUB = 256


def _moe_routed_kernel(nsub_ref, x_ref, g_ref, rrow_ref, rcol_ref, gcol_ref, wg_ref, wu_ref, wd_ref, o_ref,
                       h_sc, xg_sc, *, n_experts):
    i, e, f = pl.program_id(0), pl.program_id(1), pl.program_id(2)
    tb = x_ref.shape[0]
    n_tiles = tb // LANES

    @pl.when((e == 0) & (f == 0))
    def _():
        x = x_ref[...]
        h_sc[...] = _rms(x, g_ref[...]).astype(h_sc.dtype)
        o_ref[...] = x

    sub_iota = lax.broadcasted_iota(jnp.int32, (MOE_SUB, tb), 0)
    lane_iota = lax.broadcasted_iota(jnp.int32, (LANES, MOE_SUB), 1)

    def sub_body(s, carry):
        base = s * MOE_SUB

        @pl.when(f == 0)
        def _():
            onehot = jnp.where(rrow_ref[...] == base + sub_iota, 1.0, 0.0).astype(BF16)
            xg_sc[s] = jnp.dot(onehot, h_sc[...], preferred_element_type=F32).astype(xg_sc.dtype)

        xg = xg_sc[s]
        a = _mxu_dot(xg, wg_ref[...], False)
        u = _mxu_dot(xg, wu_ref[...], False)
        y = _mxu_dot(a * jax.nn.sigmoid(a) * u, wd_ref[...], False).astype(BF16)
        onehot_t = jnp.concatenate(
            [jnp.where(rcol_ref[:, t:t + 1] == base + lane_iota, 1.0, 0.0).astype(BF16) for t in range(n_tiles)],
            axis=0)
        back = jnp.dot(onehot_t, y, preferred_element_type=F32)
        for t in range(n_tiles):
            rows = slice(t * LANES, (t + 1) * LANES)
            o_ref[rows, :] += gcol_ref[:, t:t + 1] * back[rows, :]
        return carry

    lax.fori_loop(0, nsub_ref[i * n_experts + e], sub_body, 0)


def moe_routed(x, g, w_gate, w_up, w_down, gates, *, tf=1408):
    M, K = x.shape
    E, _, F = w_gate.shape
    tb = min(MOE_TB, M)
    nb = M // tb
    g8 = gates[:, :E].reshape(nb, tb, E)
    routed = g8 != 0.0
    rank = jnp.where(routed, jnp.cumsum(routed.astype(jnp.int32), axis=1) - 1, -1).transpose(0, 2, 1)
    n_sub = (jnp.sum(routed, axis=1, dtype=jnp.int32) + MOE_SUB - 1) // MOE_SUB
    to_col = lambda a: a.reshape(nb, E, tb // LANES, LANES).transpose(0, 1, 3, 2)
    c_spec = pl.BlockSpec((None, None, LANES, tb // LANES), lambda i, e, f, ns: (i, e, 0, 0))
    grid_spec = pltpu.PrefetchScalarGridSpec(
        num_scalar_prefetch=1,
        grid=(nb, E, F // tf),
        in_specs=[pl.BlockSpec((tb, K), lambda i, e, f, ns: (i, 0)),
                  pl.BlockSpec((1, K), lambda i, e, f, ns: (0, 0)),
                  pl.BlockSpec((None, None, 1, tb), lambda i, e, f, ns: (i, e, 0, 0)),
                  c_spec, c_spec,
                  pl.BlockSpec((None, K, tf), lambda i, e, f, ns: (e, 0, f)),
                  pl.BlockSpec((None, K, tf), lambda i, e, f, ns: (e, 0, f)),
                  pl.BlockSpec((None, tf, K), lambda i, e, f, ns: (e, f, 0))],
        out_specs=pl.BlockSpec((tb, K), lambda i, e, f, ns: (i, 0)),
        scratch_shapes=[pltpu.VMEM((tb, K), BF16), pltpu.VMEM((tb // MOE_SUB, MOE_SUB, K), BF16)],
    )
    return pl.pallas_call(
        functools.partial(_moe_routed_kernel, n_experts=E),
        grid_spec=grid_spec,
        out_shape=jax.ShapeDtypeStruct((M, K), F32),
        compiler_params=_params("arbitrary", "arbitrary", "arbitrary"),
        name="moe_routed",
    )(n_sub.reshape(-1), x, g.reshape(1, K).astype(F32), rank[:, :, None, :], to_col(rank),
      to_col(g8.transpose(0, 2, 1)), w_gate, w_up, w_down)


def _gla_scan_kernel(q_ref, k_ref, g_ref, v_ref, r_ref, gout_ref, og_ref, s_ref, st_sc, *, chunk):
    t = pl.program_id(2)
    tt, dk = q_ref.shape
    n_tiles = chunk // SUBLANES

    @pl.when(t == 0)
    def _():
        st_sc[...] = jnp.zeros(st_sc.shape, F32)

    tri = (lax.broadcasted_iota(jnp.int32, (chunk, chunk), 0)
           >= lax.broadcasted_iota(jnp.int32, (chunk, chunk), 1)).astype(F32)
    row8 = lax.broadcasted_iota(jnp.int32, (SUBLANES, dk), 0)

    def chunk_body(c, carry):
        r0 = pl.multiple_of(c * chunk, chunk)
        qc = q_ref[pl.ds(r0, chunk), :] * (dk ** -0.5)
        kc = k_ref[pl.ds(r0, chunk), :]
        vc = v_ref[pl.ds(r0, chunk), :]
        b = jnp.dot(tri, g_ref[pl.ds(r0, chunk), :], preferred_element_type=F32,
                    precision=lax.Precision.HIGHEST)
        b_last = b[chunk - 1:chunk, :]
        st = st_sc[...]
        o_inter = lax.dot_general((qc * jnp.exp(b)).astype(BF16), st.astype(BF16),
                                  (((1,), (1,)), ((), ())), preferred_element_type=F32)
        o_t = [o_inter[i * SUBLANES:(i + 1) * SUBLANES, :] for i in range(n_tiles)]
        q_t = [qc[i * SUBLANES:(i + 1) * SUBLANES, :] for i in range(n_tiles)]
        b_t = [b[i * SUBLANES:(i + 1) * SUBLANES, :] for i in range(n_tiles)]
        for s in range(chunk):
            k_s, b_s, v_s = kc[s:s + 1, :], b[s:s + 1, :], vc[s:s + 1, :]
            for i in range(s // SUBLANES, n_tiles):
                diff = b_t[i] - b_s
                if i == s // SUBLANES:
                    diff = jnp.where(row8 + i * SUBLANES >= s, diff, -jnp.inf)
                att = jnp.sum(q_t[i] * k_s * jnp.exp(diff), axis=-1, keepdims=True)
                o_t[i] = o_t[i] + att * v_s
        o = jnp.concatenate(o_t, axis=0)
        kd = kc * jnp.exp(b_last - b)
        ds = lax.dot_general(vc.astype(BF16), kd.astype(BF16), (((0,), (0,)), ((), ())),
                             preferred_element_type=F32)
        st_sc[...] = jnp.exp(b_last) * st + ds
        on = _rms(o, gout_ref[...])
        rc = r_ref[pl.ds(r0, chunk), :]
        og_ref[pl.ds(r0, chunk), :] = (on * (rc * jax.nn.sigmoid(rc))).astype(og_ref.dtype)
        return carry

    lax.fori_loop(0, tt // chunk, chunk_body, 0)

    @pl.when(t == pl.num_programs(2) - 1)
    def _():
        s_ref[...] = st_sc[...].T


def gla_scan(q, k, g, v, r, g_out, *, tt=512):
    B, T, _ = q.shape
    H, DV = g_out.shape
    DK = q.shape[2] // H
    tt = min(tt, T)
    qk_spec = pl.BlockSpec((None, tt, DK), lambda b, h, t: (b, t, h))
    v_spec = pl.BlockSpec((None, tt, DV), lambda b, h, t: (b, t, h))
    return pl.pallas_call(
        functools.partial(_gla_scan_kernel, chunk=GLA_CHUNK),
        grid=(B, H, T // tt),
        in_specs=[qk_spec, qk_spec, qk_spec, v_spec, v_spec, pl.BlockSpec((None, 1, DV), lambda b, h, t: (h, 0, 0))],
        out_specs=[v_spec, pl.BlockSpec((None, None, DK, DV), lambda b, h, t: (b, h, 0, 0))],
        out_shape=[jax.ShapeDtypeStruct((B, T, H * DV), BF16), jax.ShapeDtypeStruct((B, H, DK, DV), F32)],
        scratch_shapes=[pltpu.VMEM((DV, DK), F32)],
        compiler_params=_params("arbitrary", "arbitrary", "arbitrary"),
        name="gla_scan",
    )(q, k, g, v, r, g_out.reshape(H, 1, DV).astype(F32))


def _gla_step_kernel(qa_ref, kcol_ref, acol_ref, qk_ref, v_ref, r_ref, gout_ref, s_ref, og_ref, sn_ref):
    s = s_ref[...]
    v = v_ref[...]
    o = jnp.sum(qa_ref[...] * s, axis=1, keepdims=True) + qk_ref[...] * v
    sn_ref[...] = acol_ref[...] * s + kcol_ref[...] * v
    on = _rms(o, gout_ref[...])
    r = r_ref[...]
    og_ref[...] = on * (r * jax.nn.sigmoid(r))


def gla_step(q, k, g, v, r, g_out, s0):
    B = q.shape[0]
    H, DV = g_out.shape
    DK = q.shape[1] // H
    qh = q.reshape(B, H, DK) * (DK ** -0.5)
    kh = k.reshape(B, H, DK)
    ah = jnp.exp(g.reshape(B, H, DK))
    col = lambda a: a[..., None]
    qk = jnp.sum(qh * kh, axis=-1)[..., None, None]
    row = lambda a: a.reshape(B, H, 1, DV)
    c_spec = pl.BlockSpec((None, H, DK, 1), lambda b: (b, 0, 0, 0))
    r_spec = pl.BlockSpec((None, H, 1, DV), lambda b: (b, 0, 0, 0))
    s_spec = pl.BlockSpec((None, H, DK, DV), lambda b: (b, 0, 0, 0))
    og, sn = pl.pallas_call(
        _gla_step_kernel,
        grid=(B,),
        in_specs=[c_spec, c_spec, c_spec, pl.BlockSpec((None, H, 1, 1), lambda b: (b, 0, 0, 0)),
                  r_spec, r_spec, pl.BlockSpec((H, 1, DV), lambda b: (0, 0, 0)), s_spec],
        out_specs=[r_spec, s_spec],
        out_shape=[jax.ShapeDtypeStruct((B, H, 1, DV), F32), jax.ShapeDtypeStruct((B, H, DK, DV), F32)],
        compiler_params=_params("arbitrary"),
        name="gla_step",
    )(col(qh * ah), col(kh), col(ah), qk, row(v), row(r), g_out.reshape(H, 1, DV).astype(F32), s0)
    return og.reshape(B, H * DV), sn


def _dsa_attn_kernel(qi_ref, w_ref, q_ref, ki_ref, k_ref, vt_ref, o_ref,
                     key_sc, bias_sc, cut_sc, m_sc, l_sc, acc_sc, *, topk, n_idx_heads, n_groups, n_rep):
    tq = w_ref.shape[1]
    i = pl.program_id(1)
    n_ch = (i * tq + tq + CK - 1) // CK
    t_idx = i * tq + lax.broadcasted_iota(jnp.int32, (CK, tq), 1)
    row_iota = lax.broadcasted_iota(jnp.int32, (CK, tq), 0)

    def score_chunk(c, carry):
        r0 = pl.multiple_of(c * CK, CK)
        d = jnp.dot(ki_ref[pl.ds(r0, CK), :], qi_ref[...], preferred_element_type=F32)
        acc = jnp.zeros((CK, tq), F32)
        for h in range(n_idx_heads):
            acc = acc + jnp.maximum(d[:, h * tq:(h + 1) * tq], 0.0) * w_ref[h:h + 1, :]
        acc = jnp.where(r0 + row_iota <= t_idx, acc, -jnp.inf)
        bits = pltpu.bitcast(acc, jnp.int32)
        key_sc[pl.ds(r0, CK), :] = jnp.where(bits < 0, bits ^ INT32_MAX, bits)
        return carry

    lax.fori_loop(0, n_ch, score_chunk, 0)

    def count_rows(pred):
        def body(c, acc):
            r0 = pl.multiple_of(c * CK, CK)
            hit = jnp.where(pred(key_sc[pl.ds(r0, CK), :], r0 + row_iota), 1, 0)
            return acc + jnp.sum(hit.reshape(CK // SUBLANES, SUBLANES, tq), axis=0)
        acc = lax.fori_loop(0, n_ch, body, jnp.zeros((SUBLANES, tq), jnp.int32))
        return jnp.sum(acc, axis=0, keepdims=True)

    cnt_nonneg = count_rows(lambda key, rows: key >= 0)
    nonneg = cnt_nonneg >= topk
    thr0 = jnp.where(nonneg, 0, INT32_MIN)
    cnt0 = jnp.where(nonneg, cnt_nonneg, n_ch * CK)

    def bit_step(b, carry):
        thr, cnt_thr = carry
        cand = thr | jnp.left_shift(jnp.int32(1), 30 - b)
        cnt = count_rows(lambda key, rows: key >= cand)
        ok = cnt >= topk
        return jnp.where(ok, cand, thr), jnp.where(ok, cnt, cnt_thr)

    thr, cnt_thr = lax.fori_loop(0, 31, bit_step, (thr0, cnt0))

    excess = (cnt_thr > topk) & (thr > KEY_NEG_INF)
    cut_sc[...] = jnp.full((1, tq), INT32_MAX, jnp.int32)

    @pl.when(jnp.max(jnp.where(excess, 1, 0)) > 0)
    def _():
        need = topk - count_rows(lambda key, rows: key > thr)
        row_bits = (key_sc.shape[0] - 1).bit_length()

        def row_step(b, cut):
            cand = cut | jnp.left_shift(jnp.int32(1), row_bits - 1 - b)
            cnt = count_rows(lambda key, rows: (key == thr) & (rows < cand))
            return jnp.where(cnt < need, cand, cut)

        cut = lax.fori_loop(0, row_bits, row_step, jnp.zeros((1, tq), jnp.int32))
        cut_sc[...] = jnp.where(excess, cut, INT32_MAX)

    cut = cut_sc[...]

    def bias_chunk(c, carry):
        r0 = pl.multiple_of(c * CK, CK)
        key = key_sc[pl.ds(r0, CK), :]
        rows = r0 + row_iota
        sel = (rows <= t_idx) & ((key > thr) | ((key == thr) & (rows <= cut)))
        bias_sc[pl.ds(r0, CK), :] = jnp.where(sel, 0.0, MASK_BIAS)
        return carry

    lax.fori_loop(0, n_ch, bias_chunk, 0)

    m_sc[...] = jnp.full(m_sc.shape, MASK_BIAS, F32)
    l_sc[...] = jnp.zeros(l_sc.shape, F32)
    acc_sc[...] = jnp.zeros(acc_sc.shape, F32)

    def attn_chunk(c, carry):
        r0 = pl.multiple_of(c * CK, CK)
        bias = bias_sc[pl.ds(r0, CK), :]
        bias = jnp.concatenate([bias] * n_rep, axis=1)
        s, m_new, alpha = [], [], []
        for g in range(n_groups):
            s.append(jnp.dot(k_ref[g, pl.ds(r0, CK), :], q_ref[g], preferred_element_type=F32) + bias)
            m_old = m_sc[g]
            m_new.append(jnp.maximum(m_old, jnp.max(s[g], axis=0, keepdims=True)))
            alpha.append(jnp.exp(m_old - m_new[g]))
            m_sc[g] = m_new[g]
        for g in range(n_groups):
            p = jnp.exp(s[g] - m_new[g])
            l_sc[g] = alpha[g] * l_sc[g] + jnp.sum(p, axis=0, keepdims=True)
            pv = jnp.dot(vt_ref[g, c], p.astype(BF16), preferred_element_type=F32)
            acc_sc[g] = alpha[g] * acc_sc[g] + pv
        return carry

    lax.fori_loop(0, n_ch, attn_chunk, 0)

    heads = []
    for g in range(n_groups):
        og = acc_sc[g] / l_sc[g]
        heads += [og[:, r * tq:(r + 1) * tq] for r in range(n_rep)]
    o_ref[...] = jnp.concatenate(heads, axis=0).T.astype(o_ref.dtype)


def dsa_attention_prompt(q, k, v, qi, ki, wi, topk):
    B, T, H, hd = q.shape
    G = k.shape[2]
    R = H // G
    IH, ID = qi.shape[2], qi.shape[3]
    nb = T // TQ
    qi_l = qi.astype(BF16).reshape(B, nb, TQ, IH, ID).transpose(0, 1, 4, 3, 2).reshape(B, nb, ID, IH * TQ)
    q_l = (q * (hd ** -0.5)).astype(BF16).reshape(B, nb, TQ, G, R, hd).transpose(0, 1, 3, 5, 4, 2)
    q_l = q_l.reshape(B, nb, G, hd, R * TQ)
    w_l = wi.astype(F32).reshape(B, nb, TQ, IH).transpose(0, 1, 3, 2)
    k_l = k.astype(BF16).transpose(0, 2, 1, 3)
    vt_l = v.astype(BF16).reshape(B, T // CK, CK, G, hd).transpose(0, 3, 1, 4, 2)
    ki_l = ki.astype(BF16)
    kern = functools.partial(_dsa_attn_kernel, topk=topk, n_idx_heads=IH, n_groups=G, n_rep=R)
    return pl.pallas_call(
        kern,
        grid=(B, nb),
        in_specs=[
            pl.BlockSpec((None, None, ID, IH * TQ), lambda b, i: (b, i, 0, 0)),
            pl.BlockSpec((None, None, IH, TQ), lambda b, i: (b, i, 0, 0)),
            pl.BlockSpec((None, None, G, hd, R * TQ), lambda b, i: (b, i, 0, 0, 0)),
            pl.BlockSpec((None, T, ID), lambda b, i: (b, 0, 0)),
            pl.BlockSpec((None, G, T, hd), lambda b, i: (b, 0, 0, 0)),
            pl.BlockSpec((None, G, T // CK, hd, CK), lambda b, i: (b, 0, 0, 0, 0)),
        ],
        out_specs=pl.BlockSpec((None, TQ, H * hd), lambda b, i: (b, i, 0)),
        out_shape=jax.ShapeDtypeStruct((B, T, H * hd), BF16),
        scratch_shapes=[
            pltpu.VMEM((T, TQ), jnp.int32),
            pltpu.VMEM((T, TQ), F32),
            pltpu.VMEM((1, TQ), jnp.int32),
            pltpu.VMEM((G, 1, R * TQ), F32),
            pltpu.VMEM((G, 1, R * TQ), F32),
            pltpu.VMEM((G, hd, R * TQ), F32),
        ],
        compiler_params=_params("arbitrary", "arbitrary"),
        name="dsa_attention_prompt",
    )(qi_l, w_l, q_l, ki_l, k_l, vt_l)


def _final_norm_kernel(x_ref, g_ref, o_ref):
    o_ref[...] = _rms(x_ref[...], g_ref[...])


def final_rmsnorm(x, g, *, tm=512):
    M, K = x.shape
    tm = min(tm, M)
    return pl.pallas_call(
        _final_norm_kernel,
        grid=(M // tm,),
        in_specs=[pl.BlockSpec((tm, K), lambda i: (i, 0)), pl.BlockSpec((1, K), lambda i: (0, 0))],
        out_specs=pl.BlockSpec((tm, K), lambda i: (i, 0)),
        out_shape=jax.ShapeDtypeStruct((M, K), F32),
        compiler_params=_params("arbitrary"),
        name="final_rmsnorm",
    )(x, g.reshape(1, K).astype(F32))


def rope_partial(x, pos):
    d_rot = x.shape[-1] // ROPE_FRAC
    half = d_rot // 2
    inv = ROPE_THETA ** (-jnp.arange(half, dtype=F32) / half)
    ang = pos.astype(F32)[:, None] * inv[None, :]
    cos = jnp.cos(ang)[:, None, :]
    sin = jnp.sin(ang)[:, None, :]
    xr = x[..., :d_rot].astype(F32)
    x1, x2 = xr[..., :half], xr[..., half:]
    rot = jnp.concatenate([x1 * cos - x2 * sin, x2 * cos + x1 * sin], axis=-1)
    return jnp.concatenate([rot.astype(x.dtype), x[..., d_rot:]], axis=-1)


def _pad_cols(w, n):
    return jnp.pad(w, ((0, 0), (0, n - w.shape[1])))


def gla_layer(x, B, T, norm, w_in, w_a2, b_a, g_out, w_out, s0, hp):
    wdt = F32 if hp else BF16
    HK, HV = GLA_HEADS * GLA_DK, GLA_HEADS * GLA_DV
    n_pad = _round_up(w_in.shape[1], LANES)
    segs = [(0, HK), (HK, HK), (2 * HK, HV), (2 * HK + HV, HV), (2 * HK + 2 * HV, n_pad - 2 * HK - 2 * HV)]
    w2 = jnp.pad(w_a2, ((0, segs[-1][1] - GLA_RANK), (0, 0))).astype(wdt)
    q, k, v, r, g = norm_linear(x, norm, _pad_cols(w_in, n_pad).astype(wdt), segs=segs, tail=(w2, b_a),
                                hp=hp, name="gla_project")
    if s0 is None:
        sh = lambda a: a.reshape(B, T, a.shape[1])
        og, s_new = gla_scan(sh(q), sh(k), sh(g), sh(v), sh(r), g_out)
        og = og.reshape(B * T, HV)
    else:
        og, s_new = gla_step(q, k, g, v, r, g_out, s0)
    (x,) = norm_linear(og, None, w_out.astype(wdt), res=x, hp=hp, tm=512, name="gla_out_project")
    return x, s_new


def dsa_project(x, B, T, norm, w_in, pos, hp):
    wdt = F32 if hp else BF16
    QW, KW, IW = ATT_HEADS * HEAD_DIM, KV_HEADS * HEAD_DIM, IDX_HEADS * IDX_DIM
    n_pad = _round_up(w_in.shape[1], LANES)
    off_i = QW + 2 * KW + IW
    segs = [(0, QW), (QW, KW), (QW + KW, KW), (QW + 2 * KW, IW), (off_i, n_pad - off_i)]
    q, k, v, qi, kw = norm_linear(x, norm, _pad_cols(w_in, n_pad).astype(wdt), segs=segs, hp=hp, name="dsa_project")
    ki, wi = kw[:, :IDX_DIM], kw[:, IDX_DIM:IDX_DIM + IDX_HEADS]
    q = rope_partial(q.reshape(B, T, ATT_HEADS, HEAD_DIM), pos)
    k = rope_partial(k.reshape(B, T, KV_HEADS, HEAD_DIM), pos)
    v = v.reshape(B, T, KV_HEADS, HEAD_DIM)
    qi = rope_partial(qi.reshape(B, T, IDX_HEADS, IDX_DIM), pos)
    ki = rope_partial(ki.reshape(B, T, 1, IDX_DIM), pos)[:, :, 0]
    wi = wi.reshape(B, T, IDX_HEADS) * (IDX_HEADS ** -0.5 * IDX_DIM ** -0.5)
    return q, k, v, qi, ki, wi


def dsa_prompt_layer(x, B, T, norm, w_in, w_out):
    q, k, v, qi, ki, wi = dsa_project(x, B, T, norm, w_in, jnp.arange(T), False)
    o = dsa_attention_prompt(q, k, v, qi, ki, wi, min(TOPK_MAX, T // 4))
    (x,) = norm_linear(o.reshape(B * T, -1), None, w_out.astype(BF16), res=x, tm=512, name="dsa_out_project")
    return x, k, v, ki


def dsa_sample_layer(x, norm, w_in, w_out, cache_k, cache_v, cache_ki, page_table):
    DB, Tn = x.shape[0], 1
    pos = PAST_LEN + jnp.arange(Tn)
    q, k, v, qi, ki, wi = dsa_project(x, DB, Tn, norm, w_in, pos, True)
    L = PAST_LEN + Tn
    topk = min(TOPK_MAX, L // 4)
    with jax.default_matmul_precision("highest"):
        ki_past = cache_ki[page_table].reshape(DB, PAST_LEN, IDX_DIM)
        ki_all = jnp.concatenate([ki_past, ki], axis=1)
        dots = jnp.einsum('bthd,bsd->bths', qi, ki_all)
        sc = jnp.einsum('bth,bths->bts', wi, jax.nn.relu(dots))
        sc = jnp.where(jnp.arange(L)[None, None, :] <= pos[None, :, None], sc, -jnp.inf)
        _, idx = lax.top_k(sc, topk)
        valid = idx <= pos[None, :, None]
        in_past = idx < PAST_LEN
        pidx = jnp.minimum(idx, PAST_LEN - 1)
        phys = jax.vmap(lambda pt, ii: pt[ii])(page_table, pidx // PAGE_SIZE)
        row = phys * PAGE_SIZE + pidx % PAGE_SIZE
        nidx = jnp.clip(idx - PAST_LEN, 0, Tn - 1)
        sel = in_past[..., None, None]
        gather_rows = jax.vmap(lambda aa, ii: aa[ii])
        cached = lambda c: jnp.take(c.reshape(-1, KV_HEADS, HEAD_DIM), row, axis=0)
        kg = jnp.where(sel, cached(cache_k), gather_rows(k, nidx))
        vg = jnp.where(sel, cached(cache_v), gather_rows(v, nidx))
        qg = q.reshape(DB, Tn, KV_HEADS, ATT_HEADS // KV_HEADS, HEAD_DIM)
        s = jnp.einsum('btgrd,btkgd->btgrk', qg, kg) * (HEAD_DIM ** -0.5)
        s = jnp.where(valid[:, :, None, None, :], s, -jnp.inf)
        pr = jax.nn.softmax(s, axis=-1)
        o = jnp.einsum('btgrk,btkgd->btgrd', pr, vg).reshape(DB * Tn, ATT_HEADS * HEAD_DIM)
    (x,) = norm_linear(o, None, w_out, res=x, hp=True, name="dsa_out_project_s")
    return x, k, v, ki


def kernel(x_prompt, x_sample, state_gla, cache_k, cache_v, cache_idx_k, page_table,
           gla_norm, gla_w_in, gla_w_a2, gla_b_a, gla_out_norm, gla_w_out,
           dense_norm, dense_w_gate, dense_w_up, dense_w_down,
           dsa_norm, dsa_w_in, dsa_w_out,
           moe_norm, moe_w_router, moe_w_gate, moe_w_up, moe_w_down,
           final_norm):
    B, T, D = x_prompt.shape
    DB = x_sample.shape[0]
    xp = x_prompt.reshape(B * T, D)
    xs = x_sample.reshape(DB, D)
    sg_p, sg_s = [], []
    kp, vp, ip, ks_, vs_, is_ = [], [], [], [], [], []
    for i in range(gla_norm.shape[0] + dsa_norm.shape[0]):
        j = i // 2
        if i % 2 == 0:
            xp, sp = gla_layer(xp, B, T, gla_norm[j], gla_w_in[j], gla_w_a2[j], gla_b_a[j], gla_out_norm[j],
                               gla_w_out[j], None, False)
            xs, ss = gla_layer(xs, DB, 1, gla_norm[j], gla_w_in[j], gla_w_a2[j], gla_b_a[j], gla_out_norm[j],
                               gla_w_out[j], state_gla[j], True)
            sg_p.append(sp)
            sg_s.append(ss)
            wg, wu, wd = dense_w_gate[j][None], dense_w_up[j][None], dense_w_down[j][None]
            xp = ffn(xp, dense_norm[j], wg.astype(BF16), wu.astype(BF16), wd.astype(BF16), name="dense_ffn")
            xs = ffn(xs, dense_norm[j], wg, wu, wd, hp=True, name="dense_ffn_s")
        else:
            xp, kpn, vpn, ipn = dsa_prompt_layer(xp, B, T, dsa_norm[j], dsa_w_in[j], dsa_w_out[j])
            xs, ksn, vsn, isn = dsa_sample_layer(xs, dsa_norm[j], dsa_w_in[j], dsa_w_out[j],
                                                 cache_k[j], cache_v[j], cache_idx_k[j], page_table)
            kp.append(kpn); vp.append(vpn); ip.append(ipn)
            ks_.append(ksn); vs_.append(vsn); is_.append(isn)
            wg, wu, wd = moe_w_gate[j], moe_w_up[j], moe_w_down[j]
            xp = moe_routed(xp, moe_norm[j], wg.astype(BF16), wu.astype(BF16), wd.astype(BF16),
                            router_gates(xp, moe_norm[j], moe_w_router[j]))
            xs = ffn(xs, moe_norm[j], wg, wu, wd, router_gates(xs, moe_norm[j], moe_w_router[j]),
                     hp=True, name="moe_ffn_s")
    y_prompt = final_rmsnorm(xp, final_norm).reshape(B, T, D)
    y_sample = final_rmsnorm(xs, final_norm).reshape(DB, 1, D)
    return (y_prompt, y_sample, jnp.stack(sg_p), jnp.stack(sg_s),
            jnp.stack(kp), jnp.stack(vp), jnp.stack(ip),
            jnp.stack(ks_), jnp.stack(vs_), jnp.stack(is_))
```

```python
import functools

import jax, jax.numpy as jnp
from jax import lax
from jax.experimental import pallas as pl
from jax.experimental.pallas import tpu as pltpu

D_MODEL = 1024
PAST_LEN = 16384
PAGE_SIZE = 128
GLA_HEADS = 4
GLA_DK = D_MODEL // 2 // GLA_HEADS
GLA_DV = D_MODEL // GLA_HEADS
GLA_RANK = 16
GLA_TAU = 16.0
GLA_CHUNK = 32
ATT_HEADS = 16
KV_HEADS = 4
HEAD_DIM = D_MODEL // ATT_HEADS
IDX_HEADS = 8
IDX_DIM = 64
TOPK_MAX = 256
ROPE_THETA = 500000.0
ROPE_FRAC = 4
D_FF = 2816
N_EXPERTS = 8
EPS = 1e-6

LANES = 128
SUBLANES = 8
TQ = 128
CK = 256
MASK_BIAS = -1e30
MIN_TRUSTED_SUM = 1e-25
INT32_MIN = -2 ** 31
INT32_MAX = 2 ** 31 - 1
KEY_NEG_INF = -2139095041
VMEM_LIMIT_BYTES = 52 * 1024 * 1024
F32 = jnp.float32
BF16 = jnp.bfloat16


def _round_up(n, m):
    return -(-n // m) * m


def _params(*sem):
    return pltpu.CompilerParams(dimension_semantics=sem, vmem_limit_bytes=VMEM_LIMIT_BYTES)


def _mxu_dot(a, b, hp):
    if hp:
        return jnp.dot(a.astype(F32), b.astype(F32), preferred_element_type=F32, precision=lax.Precision.HIGHEST)
    return jnp.dot(a.astype(BF16), b.astype(BF16), preferred_element_type=F32)


def _rms(x, g):
    return x * lax.rsqrt(jnp.mean(x * x, axis=-1, keepdims=True) + EPS) * g


def _norm_linear_kernel(*refs, apply_norm, has_res, has_tail, segs, hp):
    it = iter(refs)
    x_ref = next(it)
    g_ref = next(it) if apply_norm else None
    w_ref = next(it)
    res_ref = next(it) if has_res else None
    w2_ref, b2_ref = (next(it), next(it)) if has_tail else (None, None)
    out_refs = list(it)
    x = x_ref[...]
    if apply_norm:
        x = _rms(x.astype(F32), g_ref[...])
    y = _mxu_dot(x, w_ref[...], hp)
    if has_res:
        y = y + res_ref[...]
    for n, (o_ref, (off, width)) in enumerate(zip(out_refs, segs)):
        seg = y[:, off:off + width]
        if has_tail and n == len(segs) - 1:
            z = _mxu_dot(seg, w2_ref[...], hp) + b2_ref[...]
            seg = (jnp.minimum(z, 0.0) - jnp.log(1.0 + jnp.exp(-jnp.abs(z)))) * (1.0 / GLA_TAU)
        o_ref[...] = seg.astype(o_ref.dtype)


def norm_linear(x, g, w, *, res=None, segs=None, out_dtypes=None, tail=None, hp=False, tm=256, name="norm_linear"):
    M, K = x.shape
    N = w.shape[1]
    segs = segs or [(0, N)]
    out_dtypes = out_dtypes or [F32] * len(segs)
    tm = min(tm, M)
    args, in_specs = [x], [pl.BlockSpec((tm, K), lambda i: (i, 0))]
    if g is not None:
        args.append(g.reshape(1, K).astype(F32))
        in_specs.append(pl.BlockSpec((1, K), lambda i: (0, 0)))
    args.append(w)
    in_specs.append(pl.BlockSpec((K, N), lambda i: (0, 0)))
    if res is not None:
        args.append(res)
        in_specs.append(pl.BlockSpec((tm, N), lambda i: (i, 0)))
    out_widths = [wd for _, wd in segs]
    if tail is not None:
        w2, b2 = tail
        args += [w2, b2.reshape(1, -1).astype(F32)]
        in_specs += [pl.BlockSpec(w2.shape, lambda i: (0, 0)), pl.BlockSpec((1, w2.shape[1]), lambda i: (0, 0))]
        out_widths[-1] = w2.shape[1]
    kern = functools.partial(_norm_linear_kernel, apply_norm=g is not None, has_res=res is not None,
                             has_tail=tail is not None, segs=tuple(segs), hp=hp)
    outs = pl.pallas_call(
        kern,
        grid=(M // tm,),
        in_specs=in_specs,
        out_specs=[pl.BlockSpec((tm, wd), lambda i: (i, 0)) for wd in out_widths],
        out_shape=[jax.ShapeDtypeStruct((M, wd), dt) for wd, dt in zip(out_widths, out_dtypes)],
        compiler_params=_params("arbitrary"),
        name=name,
    )(*args)
    return outs


def _ffn_kernel(*refs, has_gate, hp):
    if has_gate:
        x_ref, g_ref, gate_ref, wg_ref, wu_ref, wd_ref, o_ref, h_sc = refs
    else:
        x_ref, g_ref, wg_ref, wu_ref, wd_ref, o_ref, h_sc = refs
        gate_ref = None
    e = pl.program_id(1)
    f = pl.program_id(2)

    @pl.when((e == 0) & (f == 0))
    def _():
        x = x_ref[...]
        h_sc[...] = _rms(x, g_ref[...]).astype(h_sc.dtype)
        o_ref[...] = x

    h = h_sc[...]
    a = _mxu_dot(h, wg_ref[...], hp)
    u = _mxu_dot(h, wu_ref[...], hp)
    y = _mxu_dot(a * jax.nn.sigmoid(a) * u, wd_ref[...], hp)
    if has_gate:
        gate = gate_ref[...]
        lane = lax.broadcasted_iota(jnp.int32, gate.shape, 1)
        y = y * jnp.sum(jnp.where(lane == e, gate, 0.0), axis=-1, keepdims=True)
    o_ref[...] += y


def ffn(x, g, w_gate, w_up, w_down, gates=None, *, hp=False, tm=512, tf=1408, name="ffn"):
    M, K = x.shape
    E, _, F = w_gate.shape
    tm = min(tm, M)
    args = [x, g.reshape(1, K).astype(F32)]
    in_specs = [pl.BlockSpec((tm, K), lambda i, e, f: (i, 0)), pl.BlockSpec((1, K), lambda i, e, f: (0, 0))]
    if gates is not None:
        args.append(gates)
        in_specs.append(pl.BlockSpec((tm, gates.shape[1]), lambda i, e, f: (i, 0)))
    args += [w_gate, w_up, w_down]
    in_specs += [pl.BlockSpec((None, K, tf), lambda i, e, f: (e, 0, f)),
                 pl.BlockSpec((None, K, tf), lambda i, e, f: (e, 0, f)),
                 pl.BlockSpec((None, tf, K), lambda i, e, f: (e, f, 0))]
    return pl.pallas_call(
        functools.partial(_ffn_kernel, has_gate=gates is not None, hp=hp),
        grid=(M // tm, E, F // tf),
        in_specs=in_specs,
        out_specs=pl.BlockSpec((tm, K), lambda i, e, f: (i, 0)),
        out_shape=jax.ShapeDtypeStruct((M, K), F32),
        scratch_shapes=[pltpu.VMEM((tm, K), F32 if hp else BF16)],
        compiler_params=_params("arbitrary", "arbitrary", "arbitrary"),
        name=name,
    )(*args)


def _router_kernel(x_ref, g_ref, w_ref, o_ref, *, n_experts):
    h = _rms(x_ref[...], g_ref[...])
    logits = _mxu_dot(h, w_ref[...], True)
    lane = lax.broadcasted_iota(jnp.int32, logits.shape, 1)
    logits = jnp.where(lane < n_experts, logits, -jnp.inf)
    m1 = jnp.max(logits, axis=-1, keepdims=True)
    i1 = jnp.min(jnp.where(logits == m1, lane, LANES), axis=-1, keepdims=True)
    rest = jnp.where(lane == i1, -jnp.inf, logits)
    m2 = jnp.max(rest, axis=-1, keepdims=True)
    i2 = jnp.min(jnp.where(rest == m2, lane, LANES), axis=-1, keepdims=True)
    e2 = jnp.exp(m2 - m1)
    denom = 1.0 + e2
    o_ref[...] = jnp.where(lane == i1, 1.0 / denom, 0.0) + jnp.where(lane == i2, e2 / denom, 0.0)


def router_gates(x, g, w_router, *, tm=512):
    M, K = x.shape
    E = w_router.shape[1]
    tm = min(tm, M)
    w = jnp.pad(w_router.astype(F32), ((0, 0), (0, LANES - E)))
    return pl.pallas_call(
        functools.partial(_router_kernel, n_experts=E),
        grid=(M // tm,),
        in_specs=[pl.BlockSpec((tm, K), lambda i: (i, 0)), pl.BlockSpec((1, K), lambda i: (0, 0)),
                  pl.BlockSpec((K, LANES), lambda i: (0, 0))],
        out_specs=pl.BlockSpec((tm, LANES), lambda i: (i, 0)),
        out_shape=jax.ShapeDtypeStruct((M, LANES), F32),
        compiler_params=_params("arbitrary"),
        name="moe_router",
    )(x, g.reshape(1, K).astype(F32), w)


MOE_TB = 1024
MOE_SUB = 288


def _moe_routed_kernel(nsub_ref, x_ref, g_ref, rrow_ref, rcol_ref, gcol_ref, wg_ref, wu_ref, wd_ref, o_ref,
                       h_sc, xg_sc, y_sc, *, n_experts, n_f):
    i, e, f = pl.program_id(0), pl.program_id(1), pl.program_id(2)
    tb = x_ref.shape[0]
    n_tiles = tb // LANES

    @pl.when((e == 0) & (f == 0))
    def _():
        x = x_ref[...]
        h_sc[...] = _rms(x, g_ref[...]).astype(h_sc.dtype)
        o_ref[...] = x

    sub_iota = lax.broadcasted_iota(jnp.int32, (MOE_SUB, tb), 0)
    lane_iota = lax.broadcasted_iota(jnp.int32, (LANES, MOE_SUB), 1)

    def sub_body(s, carry):
        base = s * MOE_SUB

        @pl.when(f == 0)
        def _():
            onehot = jnp.where(rrow_ref[...] == base + sub_iota, 1.0, 0.0).astype(BF16)
            xg_sc[s] = jnp.dot(onehot, h_sc[...], preferred_element_type=F32).astype(xg_sc.dtype)

        xg = xg_sc[s]
        a = _mxu_dot(xg, wg_ref[...], False)
        u = _mxu_dot(xg, wu_ref[...], False)
        y = _mxu_dot(a * jax.nn.sigmoid(a) * u, wd_ref[...], False)

        @pl.when(f == 0)
        def _():
            y_sc[s] = y

        if n_f > 2:
            @pl.when((f > 0) & (f < n_f - 1))
            def _():
                y_sc[s] += y

        @pl.when(f == n_f - 1)
        def _():
            y_all = (y if n_f == 1 else y_sc[s] + y).astype(BF16)
            onehot_t = jnp.concatenate(
                [jnp.where(rcol_ref[:, t:t + 1] == base + lane_iota, 1.0, 0.0).astype(BF16)
                 for t in range(n_tiles)], axis=0)
            back = jnp.dot(onehot_t, y_all, preferred_element_type=F32)
            for t in range(n_tiles):
                rows = slice(t * LANES, (t + 1) * LANES)
                o_ref[rows, :] += gcol_ref[:, t:t + 1] * back[rows, :]
        return carry

    lax.fori_loop(0, nsub_ref[i * n_experts + e], sub_body, 0)


def moe_routed(x, g, w_gate, w_up, w_down, gates, *, tf=1408):
    M, K = x.shape
    E, _, F = w_gate.shape
    tb = min(MOE_TB, M)
    nb = M // tb
    g8 = gates[:, :E].reshape(nb, tb, E)
    routed = g8 != 0.0
    rank = jnp.where(routed, jnp.cumsum(routed.astype(jnp.int32), axis=1) - 1, -1).transpose(0, 2, 1)
    n_sub = (jnp.sum(routed, axis=1, dtype=jnp.int32) + MOE_SUB - 1) // MOE_SUB
    to_col = lambda a: a.reshape(nb, E, tb // LANES, LANES).transpose(0, 1, 3, 2)
    c_spec = pl.BlockSpec((None, None, LANES, tb // LANES), lambda i, e, f, ns: (i, e, 0, 0))
    grid_spec = pltpu.PrefetchScalarGridSpec(
        num_scalar_prefetch=1,
        grid=(nb, E, F // tf),
        in_specs=[pl.BlockSpec((tb, K), lambda i, e, f, ns: (i, 0)),
                  pl.BlockSpec((1, K), lambda i, e, f, ns: (0, 0)),
                  pl.BlockSpec((None, None, 1, tb), lambda i, e, f, ns: (i, e, 0, 0)),
                  c_spec, c_spec,
                  pl.BlockSpec((None, K, tf), lambda i, e, f, ns: (e, 0, f)),
                  pl.BlockSpec((None, K, tf), lambda i, e, f, ns: (e, 0, f)),
                  pl.BlockSpec((None, tf, K), lambda i, e, f, ns: (e, f, 0))],
        out_specs=pl.BlockSpec((tb, K), lambda i, e, f, ns: (i, 0)),
        scratch_shapes=[pltpu.VMEM((tb, K), BF16), pltpu.VMEM((-(-tb // MOE_SUB), MOE_SUB, K), BF16),
                        pltpu.VMEM((-(-tb // MOE_SUB), MOE_SUB, K), F32)],
    )
    return pl.pallas_call(
        functools.partial(_moe_routed_kernel, n_experts=E, n_f=F // tf),
        grid_spec=grid_spec,
        out_shape=jax.ShapeDtypeStruct((M, K), F32),
        compiler_params=_params("arbitrary", "arbitrary", "arbitrary"),
        name="moe_routed",
    )(n_sub.reshape(-1), x, g.reshape(1, K).astype(F32), rank[:, :, None, :], to_col(rank),
      to_col(g8.transpose(0, 2, 1)), w_gate, w_up, w_down)


def _gla_scan_kernel(q_ref, k_ref, g_ref, v_ref, r_ref, gout_ref, og_ref, s_ref, st_sc, *, chunk):
    t = pl.program_id(2)
    tt, dk = q_ref.shape
    n_tiles = chunk // SUBLANES

    @pl.when(t == 0)
    def _():
        st_sc[...] = jnp.zeros(st_sc.shape, F32)

    tri = (lax.broadcasted_iota(jnp.int32, (chunk, chunk), 0)
           >= lax.broadcasted_iota(jnp.int32, (chunk, chunk), 1)).astype(F32)
    row8 = lax.broadcasted_iota(jnp.int32, (SUBLANES, dk), 0)

    def chunk_body(c, carry):
        r0 = pl.multiple_of(c * chunk, chunk)
        qc = q_ref[pl.ds(r0, chunk), :] * (dk ** -0.5)
        kc = k_ref[pl.ds(r0, chunk), :]
        vc = v_ref[pl.ds(r0, chunk), :]
        b = jnp.dot(tri, g_ref[pl.ds(r0, chunk), :], preferred_element_type=F32,
                    precision=lax.Precision.HIGHEST)
        b_last = b[chunk - 1:chunk, :]
        st = st_sc[...]
        o_inter = lax.dot_general((qc * jnp.exp(b)).astype(BF16), st.astype(BF16),
                                  (((1,), (1,)), ((), ())), preferred_element_type=F32)
        o_t = [o_inter[i * SUBLANES:(i + 1) * SUBLANES, :] for i in range(n_tiles)]
        q_t = [qc[i * SUBLANES:(i + 1) * SUBLANES, :] for i in range(n_tiles)]
        b_t = [b[i * SUBLANES:(i + 1) * SUBLANES, :] for i in range(n_tiles)]
        for s in range(chunk):
            k_s, b_s, v_s = kc[s:s + 1, :], b[s:s + 1, :], vc[s:s + 1, :]
            for i in range(s // SUBLANES, n_tiles):
                diff = b_t[i] - b_s
                if i == s // SUBLANES:
                    diff = jnp.where(row8 + i * SUBLANES >= s, diff, -jnp.inf)
                att = jnp.sum(q_t[i] * k_s * jnp.exp(diff), axis=-1, keepdims=True)
                o_t[i] = o_t[i] + att * v_s
        o = jnp.concatenate(o_t, axis=0)
        kd = kc * jnp.exp(b_last - b)
        ds = lax.dot_general(vc.astype(BF16), kd.astype(BF16), (((0,), (0,)), ((), ())),
                             preferred_element_type=F32)
        st_sc[...] = jnp.exp(b_last) * st + ds
        on = _rms(o, gout_ref[...])
        rc = r_ref[pl.ds(r0, chunk), :]
        og_ref[pl.ds(r0, chunk), :] = (on * (rc * jax.nn.sigmoid(rc))).astype(og_ref.dtype)
        return carry

    lax.fori_loop(0, tt // chunk, chunk_body, 0, unroll=4)

    @pl.when(t == pl.num_programs(2) - 1)
    def _():
        s_ref[...] = st_sc[...].T


def gla_scan(q, k, g, v, r, g_out, *, tt=512):
    B, T, _ = q.shape
    H, DV = g_out.shape
    DK = q.shape[2] // H
    tt = min(tt, T)
    qk_spec = pl.BlockSpec((None, tt, DK), lambda b, h, t: (b, t, h))
    v_spec = pl.BlockSpec((None, tt, DV), lambda b, h, t: (b, t, h))
    return pl.pallas_call(
        functools.partial(_gla_scan_kernel, chunk=GLA_CHUNK),
        grid=(B, H, T // tt),
        in_specs=[qk_spec, qk_spec, qk_spec, v_spec, v_spec, pl.BlockSpec((None, 1, DV), lambda b, h, t: (h, 0, 0))],
        out_specs=[v_spec, pl.BlockSpec((None, None, DK, DV), lambda b, h, t: (b, h, 0, 0))],
        out_shape=[jax.ShapeDtypeStruct((B, T, H * DV), BF16), jax.ShapeDtypeStruct((B, H, DK, DV), F32)],
        scratch_shapes=[pltpu.VMEM((DV, DK), F32)],
        compiler_params=_params("arbitrary", "arbitrary", "arbitrary"),
        name="gla_scan",
    )(q, k, g, v, r, g_out.reshape(H, 1, DV).astype(F32))


def _gla_step_kernel(qa_ref, kcol_ref, acol_ref, qk_ref, v_ref, r_ref, gout_ref, s_ref, og_ref, sn_ref):
    s = s_ref[...]
    v = v_ref[...]
    o = jnp.sum(qa_ref[...] * s, axis=1, keepdims=True) + qk_ref[...] * v
    sn_ref[...] = acol_ref[...] * s + kcol_ref[...] * v
    on = _rms(o, gout_ref[...])
    r = r_ref[...]
    og_ref[...] = on * (r * jax.nn.sigmoid(r))


def gla_step(q, k, g, v, r, g_out, s0):
    B = q.shape[0]
    H, DV = g_out.shape
    DK = q.shape[1] // H
    qh = q.reshape(B, H, DK) * (DK ** -0.5)
    kh = k.reshape(B, H, DK)
    ah = jnp.exp(g.reshape(B, H, DK))
    col = lambda a: a[..., None]
    qk = jnp.sum(qh * kh, axis=-1)[..., None, None]
    row = lambda a: a.reshape(B, H, 1, DV)
    c_spec = pl.BlockSpec((None, H, DK, 1), lambda b: (b, 0, 0, 0))
    r_spec = pl.BlockSpec((None, H, 1, DV), lambda b: (b, 0, 0, 0))
    s_spec = pl.BlockSpec((None, H, DK, DV), lambda b: (b, 0, 0, 0))
    og, sn = pl.pallas_call(
        _gla_step_kernel,
        grid=(B,),
        in_specs=[c_spec, c_spec, c_spec, pl.BlockSpec((None, H, 1, 1), lambda b: (b, 0, 0, 0)),
                  r_spec, r_spec, pl.BlockSpec((H, 1, DV), lambda b: (0, 0, 0)), s_spec],
        out_specs=[r_spec, s_spec],
        out_shape=[jax.ShapeDtypeStruct((B, H, 1, DV), F32), jax.ShapeDtypeStruct((B, H, DK, DV), F32)],
        compiler_params=_params("arbitrary"),
        name="gla_step",
    )(col(qh * ah), col(kh), col(ah), qk, row(v), row(r), g_out.reshape(H, 1, DV).astype(F32), s0)
    return og.reshape(B, H * DV), sn


def _dsa_attn_kernel(qi_ref, w_ref, q_ref, ki_ref, k_ref, vt_ref, kmax_ref, o_ref,
                     key_sc, bias_sc, cut_sc, m_sc, l_sc, acc_sc, d_sc, s_sc, *, topk, n_idx_heads, n_groups, n_rep):
    tq = w_ref.shape[1]
    i = pl.program_id(1)
    n_ch = (i * tq + tq + CK - 1) // CK
    t_idx = i * tq + lax.broadcasted_iota(jnp.int32, (CK, tq), 1)
    row_iota = lax.broadcasted_iota(jnp.int32, (CK, tq), 0)

    last = n_ch - 1

    def idx_dots(c, slot):
        r0 = pl.multiple_of(c * CK, CK)
        d_sc[slot] = jnp.dot(ki_ref[pl.ds(r0, CK), :], qi_ref[...], preferred_element_type=F32)

    def score_keys(c, slot):
        r0 = pl.multiple_of(c * CK, CK)
        d = d_sc[slot]
        acc = jnp.zeros((CK, tq), F32)
        for h in range(n_idx_heads):
            acc = acc + jnp.maximum(d[:, h * tq:(h + 1) * tq], 0.0) * w_ref[h:h + 1, :]
        acc = jnp.where(r0 + row_iota <= t_idx, acc, -jnp.inf)
        bits = pltpu.bitcast(acc, jnp.int32)
        key_sc[pl.ds(r0, CK), :] = jnp.where(bits < 0, bits ^ INT32_MAX, bits)

    idx_dots(0, 0)

    def score_pair(c2, carry):
        c = 2 * c2
        idx_dots(jnp.minimum(c + 1, last), 1)
        score_keys(c, 0)
        idx_dots(jnp.minimum(c + 2, last), 0)
        score_keys(jnp.minimum(c + 1, last), 1)
        return carry

    lax.fori_loop(0, (n_ch + 1) // 2, score_pair, 0)

    n_pairs = (n_ch + 1) // 2

    @pl.when(n_ch % 2 == 1)
    def _():
        key_sc[pl.ds(pl.multiple_of(n_ch * CK, CK), CK), :] = jnp.full((CK, tq), KEY_NEG_INF, jnp.int32)

    row_iota2 = lax.broadcasted_iota(jnp.int32, (2 * CK, tq), 0)

    def count_rows(pred):
        def body(c2, acc):
            r0 = pl.multiple_of(c2 * 2 * CK, 2 * CK)
            hit = jnp.where(pred(key_sc[pl.ds(r0, 2 * CK), :], r0 + row_iota2), 1, 0)
            return acc + jnp.sum(hit.reshape(2 * CK // SUBLANES, SUBLANES, tq), axis=0)
        acc = lax.fori_loop(0, n_pairs, body, jnp.zeros((SUBLANES, tq), jnp.int32))
        return jnp.sum(acc, axis=0, keepdims=True)

    cnt_nonneg = count_rows(lambda key, rows: key >= 0)
    nonneg = cnt_nonneg >= topk
    thr0 = jnp.where(nonneg, 0, INT32_MIN)
    cnt0 = jnp.where(nonneg, cnt_nonneg, n_pairs * 2 * CK)

    def bit_step(b, carry):
        thr, cnt_thr = carry
        cand = thr | jnp.left_shift(jnp.int32(1), 30 - b)
        cnt = count_rows(lambda key, rows: key >= cand)
        ok = cnt >= topk
        return jnp.where(ok, cand, thr), jnp.where(ok, cnt, cnt_thr)

    thr, cnt_thr = lax.fori_loop(0, 31, bit_step, (thr0, cnt0))

    excess = (cnt_thr > topk) & (thr > KEY_NEG_INF)
    cut_sc[...] = jnp.full((1, tq), INT32_MAX, jnp.int32)

    @pl.when(jnp.max(jnp.where(excess, 1, 0)) > 0)
    def _():
        need = topk - count_rows(lambda key, rows: key > thr)
        row_bits = (key_sc.shape[0] - 1).bit_length()

        def row_step(b, cut):
            cand = cut | jnp.left_shift(jnp.int32(1), row_bits - 1 - b)
            cnt = count_rows(lambda key, rows: (key == thr) & (rows < cand))
            return jnp.where(cnt < need, cand, cut)

        cut = lax.fori_loop(0, row_bits, row_step, jnp.zeros((1, tq), jnp.int32))
        cut_sc[...] = jnp.where(excess, cut, INT32_MAX)

    cut = cut_sc[...]

    def bias_chunk(c, carry):
        r0 = pl.multiple_of(c * CK, CK)
        key = key_sc[pl.ds(r0, CK), :]
        rows = r0 + row_iota
        sel = (rows <= t_idx) & ((key > thr) | ((key == thr) & (rows <= cut)))
        bias_sc[pl.ds(r0, CK), :] = jnp.where(sel, 0.0, MASK_BIAS)
        return carry

    lax.fori_loop(0, n_ch, bias_chunk, 0)

    def tiled_bias(r0):
        return jnp.concatenate([bias_sc[pl.ds(r0, CK), :]] * n_rep, axis=1)

    def scores(g, r0):
        return jnp.dot(k_ref[g, pl.ds(r0, CK), :], q_ref[g], preferred_element_type=F32)

    l_sc[...] = jnp.zeros(l_sc.shape, F32)
    acc_sc[...] = jnp.zeros(acc_sc.shape, F32)
    shift = []
    for g in range(n_groups):
        qf = q_ref[g].astype(F32)
        q_norm = jnp.sqrt(jnp.sum(qf * qf, axis=0, keepdims=True))
        shift.append(q_norm * jnp.concatenate([kmax_ref[g]] * n_rep, axis=1))

    def stage_scores(c, slot):
        cc = jnp.minimum(c, last)
        r0 = pl.multiple_of(cc * CK, CK)
        bias = jnp.where(c <= last, tiled_bias(r0), MASK_BIAS)
        for g in range(n_groups):
            s_sc[slot, g] = scores(g, r0) + bias - shift[g]

    def accumulate(c, slot):
        cc = jnp.minimum(c, last)
        for g in range(n_groups):
            p = jnp.exp(s_sc[slot, g])
            l_sc[g] = l_sc[g] + jnp.sum(p, axis=0, keepdims=True)
            acc_sc[g] = acc_sc[g] + jnp.dot(vt_ref[g, cc], p.astype(BF16), preferred_element_type=F32)

    stage_scores(0, 0)

    def bound_pair(c2, carry):
        c = 2 * c2
        stage_scores(c + 1, 1)
        accumulate(c, 0)
        stage_scores(c + 2, 0)
        accumulate(c + 1, 1)
        return carry

    lax.fori_loop(0, (n_ch + 1) // 2, bound_pair, 0)

    @pl.when(jnp.min(l_sc[...]) < MIN_TRUSTED_SUM)
    def _():
        m_sc[...] = jnp.full(m_sc.shape, MASK_BIAS, F32)
        l_sc[...] = jnp.zeros(l_sc.shape, F32)
        acc_sc[...] = jnp.zeros(acc_sc.shape, F32)

        def online_chunk(c, carry):
            r0 = pl.multiple_of(c * CK, CK)
            bias = tiled_bias(r0)
            for g in range(n_groups):
                s = scores(g, r0) + bias
                m_old = m_sc[g]
                m_new = jnp.maximum(m_old, jnp.max(s, axis=0, keepdims=True))
                alpha = jnp.exp(m_old - m_new)
                p = jnp.exp(s - m_new)
                l_sc[g] = alpha * l_sc[g] + jnp.sum(p, axis=0, keepdims=True)
                pv = jnp.dot(vt_ref[g, c], p.astype(BF16), preferred_element_type=F32)
                acc_sc[g] = alpha * acc_sc[g] + pv
                m_sc[g] = m_new
            return carry

        lax.fori_loop(0, n_ch, online_chunk, 0)

    heads = []
    for g in range(n_groups):
        og = acc_sc[g] / l_sc[g]
        heads += [og[:, r * tq:(r + 1) * tq] for r in range(n_rep)]
    o_ref[...] = jnp.concatenate(heads, axis=0).T.astype(o_ref.dtype)


def dsa_attention_prompt(q, k, v, qi, ki, wi, topk):
    B, T, H, hd = q.shape
    G = k.shape[2]
    R = H // G
    IH, ID = qi.shape[2], qi.shape[3]
    nb = T // TQ
    assert T % (2 * CK) == 0 and T % TQ == 0, "key chunks are walked in pairs"
    qi_l = qi.astype(BF16).reshape(B, nb, TQ, IH, ID).transpose(0, 1, 4, 3, 2).reshape(B, nb, ID, IH * TQ)
    q_l = (q * (hd ** -0.5)).astype(BF16).reshape(B, nb, TQ, G, R, hd).transpose(0, 1, 3, 5, 4, 2)
    q_l = q_l.reshape(B, nb, G, hd, R * TQ)
    w_l = wi.astype(F32).reshape(B, nb, TQ, IH).transpose(0, 1, 3, 2)
    k_l = k.astype(BF16).transpose(0, 2, 1, 3)
    vt_l = v.astype(BF16).reshape(B, T // CK, CK, G, hd).transpose(0, 3, 1, 4, 2)
    ki_l = ki.astype(BF16)
    kmax = jnp.sqrt(jnp.max(jnp.sum(k_l.astype(F32) ** 2, axis=-1), axis=-1))
    kmax = jnp.broadcast_to(kmax[:, :, None, None], (B, G, 1, TQ))
    kern = functools.partial(_dsa_attn_kernel, topk=topk, n_idx_heads=IH, n_groups=G, n_rep=R)
    return pl.pallas_call(
        kern,
        grid=(B, nb),
        in_specs=[
            pl.BlockSpec((None, None, ID, IH * TQ), lambda b, i: (b, i, 0, 0)),
            pl.BlockSpec((None, None, IH, TQ), lambda b, i: (b, i, 0, 0)),
            pl.BlockSpec((None, None, G, hd, R * TQ), lambda b, i: (b, i, 0, 0, 0)),
            pl.BlockSpec((None, T, ID), lambda b, i: (b, 0, 0)),
            pl.BlockSpec((None, G, T, hd), lambda b, i: (b, 0, 0, 0)),
            pl.BlockSpec((None, G, T // CK, hd, CK), lambda b, i: (b, 0, 0, 0, 0)),
            pl.BlockSpec((None, G, 1, TQ), lambda b, i: (b, 0, 0, 0)),
        ],
        out_specs=pl.BlockSpec((None, TQ, H * hd), lambda b, i: (b, i, 0)),
        out_shape=jax.ShapeDtypeStruct((B, T, H * hd), BF16),
        scratch_shapes=[
            pltpu.VMEM((T, TQ), jnp.int32),
            pltpu.VMEM((T, TQ), F32),
            pltpu.VMEM((1, TQ), jnp.int32),
            pltpu.VMEM((G, 1, R * TQ), F32),
            pltpu.VMEM((G, 1, R * TQ), F32),
            pltpu.VMEM((G, hd, R * TQ), F32),
            pltpu.VMEM((2, CK, IH * TQ), F32),
            pltpu.VMEM((2, G, CK, R * TQ), F32),
        ],
        compiler_params=_params("arbitrary", "arbitrary"),
        name="dsa_attention_prompt",
    )(qi_l, w_l, q_l, ki_l, k_l, vt_l, kmax)


def _final_norm_kernel(x_ref, g_ref, o_ref):
    o_ref[...] = _rms(x_ref[...], g_ref[...])


def final_rmsnorm(x, g, *, tm=512):
    M, K = x.shape
    tm = min(tm, M)
    return pl.pallas_call(
        _final_norm_kernel,
        grid=(M // tm,),
        in_specs=[pl.BlockSpec((tm, K), lambda i: (i, 0)), pl.BlockSpec((1, K), lambda i: (0, 0))],
        out_specs=pl.BlockSpec((tm, K), lambda i: (i, 0)),
        out_shape=jax.ShapeDtypeStruct((M, K), F32),
        compiler_params=_params("arbitrary"),
        name="final_rmsnorm",
    )(x, g.reshape(1, K).astype(F32))


def rope_partial(x, pos):
    d_rot = x.shape[-1] // ROPE_FRAC
    half = d_rot // 2
    inv = ROPE_THETA ** (-jnp.arange(half, dtype=F32) / half)
    ang = pos.astype(F32)[:, None] * inv[None, :]
    cos = jnp.cos(ang)[:, None, :]
    sin = jnp.sin(ang)[:, None, :]
    xr = x[..., :d_rot].astype(F32)
    x1, x2 = xr[..., :half], xr[..., half:]
    rot = jnp.concatenate([x1 * cos - x2 * sin, x2 * cos + x1 * sin], axis=-1)
    return jnp.concatenate([rot.astype(x.dtype), x[..., d_rot:]], axis=-1)


def _pad_cols(w, n):
    return jnp.pad(w, ((0, 0), (0, n - w.shape[1])))


def gla_layer(x, B, T, norm, w_in, w_a2, b_a, g_out, w_out, s0, hp):
    wdt = F32 if hp else BF16
    HK, HV = GLA_HEADS * GLA_DK, GLA_HEADS * GLA_DV
    n_pad = _round_up(w_in.shape[1], LANES)
    segs = [(0, HK), (HK, HK), (2 * HK, HV), (2 * HK + HV, HV), (2 * HK + 2 * HV, n_pad - 2 * HK - 2 * HV)]
    w2 = jnp.pad(w_a2, ((0, segs[-1][1] - GLA_RANK), (0, 0))).astype(wdt)
    q, k, v, r, g = norm_linear(x, norm, _pad_cols(w_in, n_pad).astype(wdt), segs=segs, tail=(w2, b_a),
                                hp=hp, name="gla_project")
    if s0 is None:
        sh = lambda a: a.reshape(B, T, a.shape[1])
        og, s_new = gla_scan(sh(q), sh(k), sh(g), sh(v), sh(r), g_out)
        og = og.reshape(B * T, HV)
    else:
        og, s_new = gla_step(q, k, g, v, r, g_out, s0)
    (x,) = norm_linear(og, None, w_out.astype(wdt), res=x, hp=hp, tm=512, name="gla_out_project")
    return x, s_new


def dsa_project(x, B, T, norm, w_in, pos, hp):
    wdt = F32 if hp else BF16
    QW, KW, IW = ATT_HEADS * HEAD_DIM, KV_HEADS * HEAD_DIM, IDX_HEADS * IDX_DIM
    n_pad = _round_up(w_in.shape[1], LANES)
    off_i = QW + 2 * KW + IW
    segs = [(0, QW), (QW, KW), (QW + KW, KW), (QW + 2 * KW, IW), (off_i, n_pad - off_i)]
    q, k, v, qi, kw = norm_linear(x, norm, _pad_cols(w_in, n_pad).astype(wdt), segs=segs, hp=hp, name="dsa_project")
    ki, wi = kw[:, :IDX_DIM], kw[:, IDX_DIM:IDX_DIM + IDX_HEADS]
    q = rope_partial(q.reshape(B, T, ATT_HEADS, HEAD_DIM), pos)
    k = rope_partial(k.reshape(B, T, KV_HEADS, HEAD_DIM), pos)
    v = v.reshape(B, T, KV_HEADS, HEAD_DIM)
    qi = rope_partial(qi.reshape(B, T, IDX_HEADS, IDX_DIM), pos)
    ki = rope_partial(ki.reshape(B, T, 1, IDX_DIM), pos)[:, :, 0]
    wi = wi.reshape(B, T, IDX_HEADS) * (IDX_HEADS ** -0.5 * IDX_DIM ** -0.5)
    return q, k, v, qi, ki, wi


def dsa_prompt_layer(x, B, T, norm, w_in, w_out):
    q, k, v, qi, ki, wi = dsa_project(x, B, T, norm, w_in, jnp.arange(T), False)
    o = dsa_attention_prompt(q, k, v, qi, ki, wi, min(TOPK_MAX, T // 4))
    (x,) = norm_linear(o.reshape(B * T, -1), None, w_out.astype(BF16), res=x, tm=512, name="dsa_out_project")
    return x, k, v, ki


def dsa_sample_layer(x, norm, w_in, w_out, cache_k, cache_v, cache_ki, page_table):
    DB, Tn = x.shape[0], 1
    pos = PAST_LEN + jnp.arange(Tn)
    q, k, v, qi, ki, wi = dsa_project(x, DB, Tn, norm, w_in, pos, True)
    L = PAST_LEN + Tn
    topk = min(TOPK_MAX, L // 4)
    with jax.default_matmul_precision("highest"):
        ki_past = cache_ki[page_table].reshape(DB, PAST_LEN, IDX_DIM)
        ki_all = jnp.concatenate([ki_past, ki], axis=1)
        dots = jnp.einsum('bthd,bsd->bths', qi, ki_all)
        sc = jnp.einsum('bth,bths->bts', wi, jax.nn.relu(dots))
        sc = jnp.where(jnp.arange(L)[None, None, :] <= pos[None, :, None], sc, -jnp.inf)
        _, idx = lax.top_k(sc, topk)
        valid = idx <= pos[None, :, None]
        in_past = idx < PAST_LEN
        pidx = jnp.minimum(idx, PAST_LEN - 1)
        phys = jax.vmap(lambda pt, ii: pt[ii])(page_table, pidx // PAGE_SIZE)
        row = phys * PAGE_SIZE + pidx % PAGE_SIZE
        nidx = jnp.clip(idx - PAST_LEN, 0, Tn - 1)
        sel = in_past[..., None, None]
        gather_rows = jax.vmap(lambda aa, ii: aa[ii])
        cached = lambda c: jnp.take(c.reshape(-1, KV_HEADS, HEAD_DIM), row, axis=0)
        kg = jnp.where(sel, cached(cache_k), gather_rows(k, nidx))
        vg = jnp.where(sel, cached(cache_v), gather_rows(v, nidx))
        qg = q.reshape(DB, Tn, KV_HEADS, ATT_HEADS // KV_HEADS, HEAD_DIM)
        s = jnp.einsum('btgrd,btkgd->btgrk', qg, kg) * (HEAD_DIM ** -0.5)
        s = jnp.where(valid[:, :, None, None, :], s, -jnp.inf)
        pr = jax.nn.softmax(s, axis=-1)
        o = jnp.einsum('btgrk,btkgd->btgrd', pr, vg).reshape(DB * Tn, ATT_HEADS * HEAD_DIM)
    (x,) = norm_linear(o, None, w_out, res=x, hp=True, name="dsa_out_project_s")
    return x, k, v, ki


def kernel(x_prompt, x_sample, state_gla, cache_k, cache_v, cache_idx_k, page_table,
           gla_norm, gla_w_in, gla_w_a2, gla_b_a, gla_out_norm, gla_w_out,
           dense_norm, dense_w_gate, dense_w_up, dense_w_down,
           dsa_norm, dsa_w_in, dsa_w_out,
           moe_norm, moe_w_router, moe_w_gate, moe_w_up, moe_w_down,
           final_norm):
    B, T, D = x_prompt.shape
    DB = x_sample.shape[0]
    xp = x_prompt.reshape(B * T, D)
    xs = x_sample.reshape(DB, D)
    sg_p, sg_s = [], []
    kp, vp, ip, ks_, vs_, is_ = [], [], [], [], [], []
    for i in range(gla_norm.shape[0] + dsa_norm.shape[0]):
        j = i // 2
        if i % 2 == 0:
            xp, sp = gla_layer(xp, B, T, gla_norm[j], gla_w_in[j], gla_w_a2[j], gla_b_a[j], gla_out_norm[j],
                               gla_w_out[j], None, False)
            xs, ss = gla_layer(xs, DB, 1, gla_norm[j], gla_w_in[j], gla_w_a2[j], gla_b_a[j], gla_out_norm[j],
                               gla_w_out[j], state_gla[j], True)
            sg_p.append(sp)
            sg_s.append(ss)
            wg, wu, wd = dense_w_gate[j][None], dense_w_up[j][None], dense_w_down[j][None]
            xp = ffn(xp, dense_norm[j], wg.astype(BF16), wu.astype(BF16), wd.astype(BF16), name="dense_ffn")
            xs = ffn(xs, dense_norm[j], wg, wu, wd, hp=True, name="dense_ffn_s")
        else:
            xp, kpn, vpn, ipn = dsa_prompt_layer(xp, B, T, dsa_norm[j], dsa_w_in[j], dsa_w_out[j])
            xs, ksn, vsn, isn = dsa_sample_layer(xs, dsa_norm[j], dsa_w_in[j], dsa_w_out[j],
                                                 cache_k[j], cache_v[j], cache_idx_k[j], page_table)
            kp.append(kpn); vp.append(vpn); ip.append(ipn)
            ks_.append(ksn); vs_.append(vsn); is_.append(isn)
            wg, wu, wd = moe_w_gate[j], moe_w_up[j], moe_w_down[j]
            xp = moe_routed(xp, moe_norm[j], wg.astype(BF16), wu.astype(BF16), wd.astype(BF16),
                            router_gates(xp, moe_norm[j], moe_w_router[j]))
            xs = ffn(xs, moe_norm[j], wg, wu, wd, router_gates(xs, moe_norm[j], moe_w_router[j]),
                     hp=True, name="moe_ffn_s")
    y_prompt = final_rmsnorm(xp, final_norm).reshape(B, T, D)
    y_sample = final_rmsnorm(xs, final_norm).reshape(DB, 1, D)
    return (y_prompt, y_sample, jnp.stack(sg_p), jnp.stack(sg_s),
            jnp.stack(kp), jnp.stack(vp), jnp.stack(ip),
            jnp.stack(ks_), jnp.stack(vs_), jnp.stack(is_))
```

```python
import functools

import jax, jax.numpy as jnp
from jax import lax
from jax.experimental import pallas as pl
from jax.experimental.pallas import tpu as pltpu

D_MODEL = 1024
PAST_LEN = 16384
PAGE_SIZE = 128
GLA_HEADS = 4
GLA_DK = D_MODEL // 2 // GLA_HEADS
GLA_DV = D_MODEL // GLA_HEADS
GLA_RANK = 16
GLA_TAU = 16.0
GLA_CHUNK = 32
ATT_HEADS = 16
KV_HEADS = 4
HEAD_DIM = D_MODEL // ATT_HEADS
IDX_HEADS = 8
IDX_DIM = 64
TOPK_MAX = 256
ROPE_THETA = 500000.0
ROPE_FRAC = 4
D_FF = 2816
N_EXPERTS = 8
EPS = 1e-6

LANES = 128
SUBLANES = 8
TQ = 128
CK = 256
MASK_BIAS = -1e30
MIN_TRUSTED_SUM = 1e-25
INT32_MIN = -2 ** 31
INT32_MAX = 2 ** 31 - 1
KEY_NEG_INF = -2139095041
VMEM_LIMIT_BYTES = 52 * 1024 * 1024
F32 = jnp.float32
BF16 = jnp.bfloat16


def _round_up(n, m):
    return -(-n // m) * m


def _params(*sem):
    return pltpu.CompilerParams(dimension_semantics=sem, vmem_limit_bytes=VMEM_LIMIT_BYTES)


def _mxu_dot(a, b, hp):
    if hp:
        return jnp.dot(a.astype(F32), b.astype(F32), preferred_element_type=F32, precision=lax.Precision.HIGHEST)
    return jnp.dot(a.astype(BF16), b.astype(BF16), preferred_element_type=F32)


def _rms(x, g):
    return x * lax.rsqrt(jnp.mean(x * x, axis=-1, keepdims=True) + EPS) * g


def _norm_linear_kernel(*refs, apply_norm, has_res, has_tail, segs, hp):
    it = iter(refs)
    x_ref = next(it)
    g_ref = next(it) if apply_norm else None
    w_ref = next(it)
    res_ref = next(it) if has_res else None
    w2_ref, b2_ref = (next(it), next(it)) if has_tail else (None, None)
    out_refs = list(it)
    x = x_ref[...]
    if apply_norm:
        x = _rms(x.astype(F32), g_ref[...])
    y = _mxu_dot(x, w_ref[...], hp)
    if has_res:
        y = y + res_ref[...]
    for n, (o_ref, (off, width)) in enumerate(zip(out_refs, segs)):
        seg = y[:, off:off + width]
        if has_tail and n == len(segs) - 1:
            z = _mxu_dot(seg, w2_ref[...], hp) + b2_ref[...]
            seg = (jnp.minimum(z, 0.0) - jnp.log(1.0 + jnp.exp(-jnp.abs(z)))) * (1.0 / GLA_TAU)
        o_ref[...] = seg.astype(o_ref.dtype)


def norm_linear(x, g, w, *, res=None, segs=None, out_dtypes=None, tail=None, hp=False, tm=256, name="norm_linear"):
    M, K = x.shape
    N = w.shape[1]
    segs = segs or [(0, N)]
    out_dtypes = out_dtypes or [F32] * len(segs)
    tm = min(tm, M)
    args, in_specs = [x], [pl.BlockSpec((tm, K), lambda i: (i, 0))]
    if g is not None:
        args.append(g.reshape(1, K).astype(F32))
        in_specs.append(pl.BlockSpec((1, K), lambda i: (0, 0)))
    args.append(w)
    in_specs.append(pl.BlockSpec((K, N), lambda i: (0, 0)))
    if res is not None:
        args.append(res)
        in_specs.append(pl.BlockSpec((tm, N), lambda i: (i, 0)))
    out_widths = [wd for _, wd in segs]
    if tail is not None:
        w2, b2 = tail
        args += [w2, b2.reshape(1, -1).astype(F32)]
        in_specs += [pl.BlockSpec(w2.shape, lambda i: (0, 0)), pl.BlockSpec((1, w2.shape[1]), lambda i: (0, 0))]
        out_widths[-1] = w2.shape[1]
    kern = functools.partial(_norm_linear_kernel, apply_norm=g is not None, has_res=res is not None,
                             has_tail=tail is not None, segs=tuple(segs), hp=hp)
    outs = pl.pallas_call(
        kern,
        grid=(M // tm,),
        in_specs=in_specs,
        out_specs=[pl.BlockSpec((tm, wd), lambda i: (i, 0)) for wd in out_widths],
        out_shape=[jax.ShapeDtypeStruct((M, wd), dt) for wd, dt in zip(out_widths, out_dtypes)],
        compiler_params=_params("arbitrary"),
        name=name,
    )(*args)
    return outs


def _ffn_kernel(*refs, has_gate, hp):
    if has_gate:
        x_ref, g_ref, gate_ref, wg_ref, wu_ref, wd_ref, o_ref, h_sc = refs
    else:
        x_ref, g_ref, wg_ref, wu_ref, wd_ref, o_ref, h_sc = refs
        gate_ref = None
    e = pl.program_id(1)
    f = pl.program_id(2)

    @pl.when((e == 0) & (f == 0))
    def _():
        x = x_ref[...]
        h_sc[...] = _rms(x, g_ref[...]).astype(h_sc.dtype)
        o_ref[...] = x

    h = h_sc[...]
    a = _mxu_dot(h, wg_ref[...], hp)
    u = _mxu_dot(h, wu_ref[...], hp)
    y = _mxu_dot(a * jax.nn.sigmoid(a) * u, wd_ref[...], hp)
    if has_gate:
        gate = gate_ref[...]
        lane = lax.broadcasted_iota(jnp.int32, gate.shape, 1)
        y = y * jnp.sum(jnp.where(lane == e, gate, 0.0), axis=-1, keepdims=True)
    o_ref[...] += y


def ffn(x, g, w_gate, w_up, w_down, gates=None, *, hp=False, tm=512, tf=1408, name="ffn"):
    M, K = x.shape
    E, _, F = w_gate.shape
    tm = min(tm, M)
    args = [x, g.reshape(1, K).astype(F32)]
    in_specs = [pl.BlockSpec((tm, K), lambda i, e, f: (i, 0)), pl.BlockSpec((1, K), lambda i, e, f: (0, 0))]
    if gates is not None:
        args.append(gates)
        in_specs.append(pl.BlockSpec((tm, gates.shape[1]), lambda i, e, f: (i, 0)))
    args += [w_gate, w_up, w_down]
    in_specs += [pl.BlockSpec((None, K, tf), lambda i, e, f: (e, 0, f)),
                 pl.BlockSpec((None, K, tf), lambda i, e, f: (e, 0, f)),
                 pl.BlockSpec((None, tf, K), lambda i, e, f: (e, f, 0))]
    return pl.pallas_call(
        functools.partial(_ffn_kernel, has_gate=gates is not None, hp=hp),
        grid=(M // tm, E, F // tf),
        in_specs=in_specs,
        out_specs=pl.BlockSpec((tm, K), lambda i, e, f: (i, 0)),
        out_shape=jax.ShapeDtypeStruct((M, K), F32),
        scratch_shapes=[pltpu.VMEM((tm, K), F32 if hp else BF16)],
        compiler_params=_params("arbitrary", "arbitrary", "arbitrary"),
        name=name,
    )(*args)


def _router_kernel(x_ref, g_ref, w_ref, o_ref, *, n_experts):
    h = _rms(x_ref[...], g_ref[...])
    logits = _mxu_dot(h, w_ref[...], True)
    lane = lax.broadcasted_iota(jnp.int32, logits.shape, 1)
    logits = jnp.where(lane < n_experts, logits, -jnp.inf)
    m1 = jnp.max(logits, axis=-1, keepdims=True)
    i1 = jnp.min(jnp.where(logits == m1, lane, LANES), axis=-1, keepdims=True)
    rest = jnp.where(lane == i1, -jnp.inf, logits)
    m2 = jnp.max(rest, axis=-1, keepdims=True)
    i2 = jnp.min(jnp.where(rest == m2, lane, LANES), axis=-1, keepdims=True)
    e2 = jnp.exp(m2 - m1)
    denom = 1.0 + e2
    o_ref[...] = jnp.where(lane == i1, 1.0 / denom, 0.0) + jnp.where(lane == i2, e2 / denom, 0.0)


def router_gates(x, g, w_router, *, tm=512):
    M, K = x.shape
    E = w_router.shape[1]
    tm = min(tm, M)
    w = jnp.pad(w_router.astype(F32), ((0, 0), (0, LANES - E)))
    return pl.pallas_call(
        functools.partial(_router_kernel, n_experts=E),
        grid=(M // tm,),
        in_specs=[pl.BlockSpec((tm, K), lambda i: (i, 0)), pl.BlockSpec((1, K), lambda i: (0, 0)),
                  pl.BlockSpec((K, LANES), lambda i: (0, 0))],
        out_specs=pl.BlockSpec((tm, LANES), lambda i: (i, 0)),
        out_shape=jax.ShapeDtypeStruct((M, LANES), F32),
        compiler_params=_params("arbitrary"),
        name="moe_router",
    )(x, g.reshape(1, K).astype(F32), w)


MOE_TB = 1024
MOE_SUB = 288


def _moe_routed_kernel(nsub_ref, x_ref, g_ref, gout_ref, rrow_ref, rcol_ref, gcol_ref, wg_ref, wu_ref, wd_ref, o_ref,
                       h_sc, xg_sc, y_sc, *, n_experts, n_f, norm_out):
    i, e, f = pl.program_id(0), pl.program_id(1), pl.program_id(2)
    tb = x_ref.shape[0]
    n_tiles = tb // LANES

    @pl.when((e == 0) & (f == 0))
    def _():
        x = x_ref[...]
        h_sc[...] = _rms(x, g_ref[...]).astype(h_sc.dtype)
        o_ref[...] = x

    sub_iota = lax.broadcasted_iota(jnp.int32, (MOE_SUB, tb), 0)
    lane_iota = lax.broadcasted_iota(jnp.int32, (LANES, MOE_SUB), 1)

    def sub_body(s, carry):
        base = s * MOE_SUB

        @pl.when(f == 0)
        def _():
            onehot = jnp.where(rrow_ref[...] == base + sub_iota, 1.0, 0.0).astype(BF16)
            xg_sc[s] = jnp.dot(onehot, h_sc[...], preferred_element_type=F32).astype(xg_sc.dtype)

        xg = xg_sc[s]
        a = _mxu_dot(xg, wg_ref[...], False)
        u = _mxu_dot(xg, wu_ref[...], False)
        y = _mxu_dot(a * jax.nn.sigmoid(a) * u, wd_ref[...], False)

        @pl.when(f == 0)
        def _():
            y_sc[s] = y

        if n_f > 2:
            @pl.when((f > 0) & (f < n_f - 1))
            def _():
                y_sc[s] += y

        @pl.when(f == n_f - 1)
        def _():
            y_all = (y if n_f == 1 else y_sc[s] + y).astype(BF16)
            onehot_t = jnp.concatenate(
                [jnp.where(rcol_ref[:, t:t + 1] == base + lane_iota, 1.0, 0.0).astype(BF16)
                 for t in range(n_tiles)], axis=0)
            back = jnp.dot(onehot_t, y_all, preferred_element_type=F32)
            for t in range(n_tiles):
                rows = slice(t * LANES, (t + 1) * LANES)
                o_ref[rows, :] += gcol_ref[:, t:t + 1] * back[rows, :]
        return carry

    lax.fori_loop(0, nsub_ref[i * n_experts + e], sub_body, 0)

    if norm_out:
        @pl.when((e == n_experts - 1) & (f == n_f - 1))
        def _():
            o_ref[...] = _rms(o_ref[...], gout_ref[...])


def moe_routed(x, g, w_gate, w_up, w_down, gates, out_gain=None, *, tf=1408):
    M, K = x.shape
    E, _, F = w_gate.shape
    tb = min(MOE_TB, M)
    nb = M // tb
    g8 = gates[:, :E].reshape(nb, tb, E)
    routed = g8 != 0.0
    rank = jnp.where(routed, jnp.cumsum(routed.astype(jnp.int32), axis=1) - 1, -1).transpose(0, 2, 1)
    n_sub = (jnp.sum(routed, axis=1, dtype=jnp.int32) + MOE_SUB - 1) // MOE_SUB
    to_col = lambda a: a.reshape(nb, E, tb // LANES, LANES).transpose(0, 1, 3, 2)
    c_spec = pl.BlockSpec((None, None, LANES, tb // LANES), lambda i, e, f, ns: (i, e, 0, 0))
    grid_spec = pltpu.PrefetchScalarGridSpec(
        num_scalar_prefetch=1,
        grid=(nb, E, F // tf),
        in_specs=[pl.BlockSpec((tb, K), lambda i, e, f, ns: (i, 0)),
                  pl.BlockSpec((1, K), lambda i, e, f, ns: (0, 0)),
                  pl.BlockSpec((1, K), lambda i, e, f, ns: (0, 0)),
                  pl.BlockSpec((None, None, 1, tb), lambda i, e, f, ns: (i, e, 0, 0)),
                  c_spec, c_spec,
                  pl.BlockSpec((None, K, tf), lambda i, e, f, ns: (e, 0, f)),
                  pl.BlockSpec((None, K, tf), lambda i, e, f, ns: (e, 0, f)),
                  pl.BlockSpec((None, tf, K), lambda i, e, f, ns: (e, f, 0))],
        out_specs=pl.BlockSpec((tb, K), lambda i, e, f, ns: (i, 0)),
        scratch_shapes=[pltpu.VMEM((tb, K), BF16), pltpu.VMEM((-(-tb // MOE_SUB), MOE_SUB, K), BF16),
                        pltpu.VMEM((-(-tb // MOE_SUB), MOE_SUB, K), F32)],
    )
    return pl.pallas_call(
        functools.partial(_moe_routed_kernel, n_experts=E, n_f=F // tf, norm_out=out_gain is not None),
        grid_spec=grid_spec,
        out_shape=jax.ShapeDtypeStruct((M, K), F32),
        compiler_params=_params("arbitrary", "arbitrary", "arbitrary"),
        name="moe_routed",
    )(n_sub.reshape(-1), x, g.reshape(1, K).astype(F32),
      (jnp.ones((K,), F32) if out_gain is None else out_gain).reshape(1, K).astype(F32), rank[:, :, None, :], to_col(rank),
      to_col(g8.transpose(0, 2, 1)), w_gate, w_up, w_down)


def _gla_scan_kernel(q_ref, k_ref, g_ref, v_ref, r_ref, gout_ref, og_ref, s_ref, st_sc, *, chunk):
    t = pl.program_id(2)
    tt, dk = q_ref.shape
    n_tiles = chunk // SUBLANES

    @pl.when(t == 0)
    def _():
        st_sc[...] = jnp.zeros(st_sc.shape, F32)

    tri = (lax.broadcasted_iota(jnp.int32, (chunk, chunk), 0)
           >= lax.broadcasted_iota(jnp.int32, (chunk, chunk), 1)).astype(F32)
    row8 = lax.broadcasted_iota(jnp.int32, (SUBLANES, dk), 0)

    def chunk_body(c, carry):
        r0 = pl.multiple_of(c * chunk, chunk)
        qc = q_ref[pl.ds(r0, chunk), :] * (dk ** -0.5)
        kc = k_ref[pl.ds(r0, chunk), :]
        vc = v_ref[pl.ds(r0, chunk), :]
        b = jnp.dot(tri, g_ref[pl.ds(r0, chunk), :], preferred_element_type=F32,
                    precision=lax.Precision.HIGHEST)
        b_last = b[chunk - 1:chunk, :]
        st = st_sc[...]
        o_inter = lax.dot_general((qc * jnp.exp(b)).astype(BF16), st.astype(BF16),
                                  (((1,), (1,)), ((), ())), preferred_element_type=F32)
        o_t = [o_inter[i * SUBLANES:(i + 1) * SUBLANES, :] for i in range(n_tiles)]
        q_t = [qc[i * SUBLANES:(i + 1) * SUBLANES, :] for i in range(n_tiles)]
        b_t = [b[i * SUBLANES:(i + 1) * SUBLANES, :] for i in range(n_tiles)]
        for s in range(chunk):
            k_s, b_s, v_s = kc[s:s + 1, :], b[s:s + 1, :], vc[s:s + 1, :]
            for i in range(s // SUBLANES, n_tiles):
                diff = b_t[i] - b_s
                if i == s // SUBLANES:
                    diff = jnp.where(row8 + i * SUBLANES >= s, diff, -jnp.inf)
                att = jnp.sum(q_t[i] * k_s * jnp.exp(diff), axis=-1, keepdims=True)
                o_t[i] = o_t[i] + att * v_s
        o = jnp.concatenate(o_t, axis=0)
        kd = kc * jnp.exp(b_last - b)
        ds = lax.dot_general(vc.astype(BF16), kd.astype(BF16), (((0,), (0,)), ((), ())),
                             preferred_element_type=F32)
        st_sc[...] = jnp.exp(b_last) * st + ds
        on = _rms(o, gout_ref[...])
        rc = r_ref[pl.ds(r0, chunk), :]
        og_ref[pl.ds(r0, chunk), :] = (on * (rc * jax.nn.sigmoid(rc))).astype(og_ref.dtype)
        return carry

    lax.fori_loop(0, tt // chunk, chunk_body, 0, unroll=8)

    @pl.when(t == pl.num_programs(2) - 1)
    def _():
        s_ref[...] = st_sc[...].T


def gla_scan(q, k, g, v, r, g_out, *, tt=512):
    B, T, _ = q.shape
    H, DV = g_out.shape
    DK = q.shape[2] // H
    tt = min(tt, T)
    qk_spec = pl.BlockSpec((None, tt, DK), lambda b, h, t: (b, t, h))
    v_spec = pl.BlockSpec((None, tt, DV), lambda b, h, t: (b, t, h))
    return pl.pallas_call(
        functools.partial(_gla_scan_kernel, chunk=GLA_CHUNK),
        grid=(B, H, T // tt),
        in_specs=[qk_spec, qk_spec, qk_spec, v_spec, v_spec, pl.BlockSpec((None, 1, DV), lambda b, h, t: (h, 0, 0))],
        out_specs=[v_spec, pl.BlockSpec((None, None, DK, DV), lambda b, h, t: (b, h, 0, 0))],
        out_shape=[jax.ShapeDtypeStruct((B, T, H * DV), BF16), jax.ShapeDtypeStruct((B, H, DK, DV), F32)],
        scratch_shapes=[pltpu.VMEM((DV, DK), F32)],
        compiler_params=_params("arbitrary", "arbitrary", "arbitrary"),
        name="gla_scan",
    )(q, k, g, v, r, g_out.reshape(H, 1, DV).astype(F32))


def _gla_step_kernel(qa_ref, kcol_ref, acol_ref, qk_ref, v_ref, r_ref, gout_ref, s_ref, og_ref, sn_ref):
    s = s_ref[...]
    v = v_ref[...]
    o = jnp.sum(qa_ref[...] * s, axis=1, keepdims=True) + qk_ref[...] * v
    sn_ref[...] = acol_ref[...] * s + kcol_ref[...] * v
    on = _rms(o, gout_ref[...])
    r = r_ref[...]
    og_ref[...] = on * (r * jax.nn.sigmoid(r))


def gla_step(q, k, g, v, r, g_out, s0):
    B = q.shape[0]
    H, DV = g_out.shape
    DK = q.shape[1] // H
    qh = q.reshape(B, H, DK) * (DK ** -0.5)
    kh = k.reshape(B, H, DK)
    ah = jnp.exp(g.reshape(B, H, DK))
    col = lambda a: a[..., None]
    qk = jnp.sum(qh * kh, axis=-1)[..., None, None]
    row = lambda a: a.reshape(B, H, 1, DV)
    c_spec = pl.BlockSpec((None, H, DK, 1), lambda b: (b, 0, 0, 0))
    r_spec = pl.BlockSpec((None, H, 1, DV), lambda b: (b, 0, 0, 0))
    s_spec = pl.BlockSpec((None, H, DK, DV), lambda b: (b, 0, 0, 0))
    og, sn = pl.pallas_call(
        _gla_step_kernel,
        grid=(B,),
        in_specs=[c_spec, c_spec, c_spec, pl.BlockSpec((None, H, 1, 1), lambda b: (b, 0, 0, 0)),
                  r_spec, r_spec, pl.BlockSpec((H, 1, DV), lambda b: (0, 0, 0)), s_spec],
        out_specs=[r_spec, s_spec],
        out_shape=[jax.ShapeDtypeStruct((B, H, 1, DV), F32), jax.ShapeDtypeStruct((B, H, DK, DV), F32)],
        compiler_params=_params("arbitrary"),
        name="gla_step",
    )(col(qh * ah), col(kh), col(ah), qk, row(v), row(r), g_out.reshape(H, 1, DV).astype(F32), s0)
    return og.reshape(B, H * DV), sn


def _score_keys(scores):
    bits = pltpu.bitcast(scores, jnp.int32)
    return jnp.where(bits < 0, bits ^ INT32_MAX, bits)


def _topk_threshold(count_rows, n_rows, topk, max_rows, cut_sc):
    lanes = cut_sc.shape[1]
    cnt_nonneg = count_rows(lambda key, rows: key >= 0)
    nonneg = cnt_nonneg >= topk
    thr0 = jnp.where(nonneg, 0, INT32_MIN)
    cnt0 = jnp.where(nonneg, cnt_nonneg, n_rows)

    def bit_step(b, carry):
        thr, cnt_thr = carry
        cand = thr | jnp.left_shift(jnp.int32(1), 30 - b)
        cnt = count_rows(lambda key, rows: key >= cand)
        ok = cnt >= topk
        return jnp.where(ok, cand, thr), jnp.where(ok, cnt, cnt_thr)

    thr, cnt_thr = lax.fori_loop(0, 31, bit_step, (thr0, cnt0))

    excess = (cnt_thr > topk) & (thr > KEY_NEG_INF)
    cut_sc[...] = jnp.full((1, lanes), INT32_MAX, jnp.int32)

    @pl.when(jnp.max(jnp.where(excess, 1, 0)) > 0)
    def _():
        need = topk - count_rows(lambda key, rows: key > thr)
        row_bits = (max_rows - 1).bit_length()

        def row_step(b, cut):
            cand = cut | jnp.left_shift(jnp.int32(1), row_bits - 1 - b)
            cnt = count_rows(lambda key, rows: (key == thr) & (rows < cand))
            return jnp.where(cnt < need, cand, cut)

        cut = lax.fori_loop(0, row_bits, row_step, jnp.zeros((1, lanes), jnp.int32))
        cut_sc[...] = jnp.where(excess, cut, INT32_MAX)

    return thr, cut_sc[...]


def _topk_mask_kernel(sc_ref, o_ref, key_sc, cut_sc, *, topk, n_valid, chunk):
    n_rows, lanes = sc_ref.shape
    n_chunks = n_rows // chunk
    row_iota = lax.broadcasted_iota(jnp.int32, (chunk, lanes), 0)

    def to_keys(c, carry):
        r0 = pl.multiple_of(c * chunk, chunk)
        key_sc[pl.ds(r0, chunk), :] = _score_keys(sc_ref[pl.ds(r0, chunk), :])
        return carry

    lax.fori_loop(0, n_chunks, to_keys, 0)

    def count_rows(pred):
        def body(c, acc):
            r0 = pl.multiple_of(c * chunk, chunk)
            hit = jnp.where(pred(key_sc[pl.ds(r0, chunk), :], r0 + row_iota), 1, 0)
            return acc + jnp.sum(hit.reshape(chunk // SUBLANES, SUBLANES, lanes), axis=0)
        acc = lax.fori_loop(0, n_chunks, body, jnp.zeros((SUBLANES, lanes), jnp.int32))
        return jnp.sum(acc, axis=0, keepdims=True)

    thr, cut = _topk_threshold(count_rows, n_rows, topk, n_rows, cut_sc)

    def write(c, carry):
        r0 = pl.multiple_of(c * chunk, chunk)
        key = key_sc[pl.ds(r0, chunk), :]
        rows = r0 + row_iota
        sel = (rows < n_valid) & ((key > thr) | ((key == thr) & (rows <= cut)))
        o_ref[pl.ds(r0, chunk), :] = jnp.where(sel, 1.0, 0.0)
        return carry

    lax.fori_loop(0, n_chunks, write, 0)


def topk_indices(scores, topk, *, chunk=512):
    Q, L = scores.shape
    rows = _round_up(L, chunk)
    sc = jnp.pad(scores.T, ((0, rows - L), (0, LANES - Q)), constant_values=-jnp.inf)
    mask = pl.pallas_call(
        functools.partial(_topk_mask_kernel, topk=topk, n_valid=L, chunk=chunk),
        out_shape=jax.ShapeDtypeStruct((rows, LANES), F32),
        scratch_shapes=[pltpu.VMEM((rows, LANES), jnp.int32), pltpu.VMEM((1, LANES), jnp.int32)],
        compiler_params=pltpu.CompilerParams(vmem_limit_bytes=VMEM_LIMIT_BYTES),
        name="topk_mask",
    )(sc)
    sel = mask[:L, :Q].T > 0.5
    return jax.vmap(lambda m: jnp.nonzero(m, size=topk, fill_value=0)[0])(sel).astype(jnp.int32)


def _dsa_attn_kernel(qi_in_ref, w_ref, q_in_ref, ki_ref, k_ref, vt_ref, kmax_ref, o_ref,
                     key_sc, bias_sc, cut_sc, m_sc, l_sc, acc_sc, d_sc, s_sc, qi_ref, q_ref,
                     *, topk, n_idx_heads, n_groups, n_rep):
    tq = w_ref.shape[1]
    i = pl.program_id(1)

    def heads_to_lanes(src_ref, n_heads, dim, store):
        per_slab = LANES // dim
        for j in range(n_heads // per_slab):
            slab = src_ref[:, j * LANES:(j + 1) * LANES].astype(F32).T
            for u in range(per_slab):
                store(j * per_slab + u, slab[u * dim:(u + 1) * dim, :].astype(BF16))

    def store_qi(h, x):
        qi_ref[:, h * tq:(h + 1) * tq] = x

    def store_q(h, x):
        q_ref[h // n_rep, :, (h % n_rep) * tq:(h % n_rep + 1) * tq] = x

    heads_to_lanes(qi_in_ref, n_idx_heads, qi_ref.shape[0], store_qi)
    heads_to_lanes(q_in_ref, n_groups * n_rep, q_ref.shape[1], store_q)

    n_ch = (i * tq + tq + CK - 1) // CK
    t_idx = i * tq + lax.broadcasted_iota(jnp.int32, (CK, tq), 1)
    row_iota = lax.broadcasted_iota(jnp.int32, (CK, tq), 0)

    last = n_ch - 1

    def idx_dots(c, slot):
        r0 = pl.multiple_of(c * CK, CK)
        d_sc[slot] = jnp.dot(ki_ref[pl.ds(r0, CK), :], qi_ref[...], preferred_element_type=F32)

    def score_keys(c, slot):
        r0 = pl.multiple_of(c * CK, CK)
        d = d_sc[slot]
        acc = jnp.zeros((CK, tq), F32)
        for h in range(n_idx_heads):
            acc = acc + jnp.maximum(d[:, h * tq:(h + 1) * tq], 0.0) * w_ref[h:h + 1, :]
        acc = jnp.where(r0 + row_iota <= t_idx, acc, -jnp.inf)
        bits = pltpu.bitcast(acc, jnp.int32)
        key_sc[pl.ds(r0, CK), :] = jnp.where(bits < 0, bits ^ INT32_MAX, bits)

    idx_dots(0, 0)

    def score_pair(c2, carry):
        c = 2 * c2
        idx_dots(jnp.minimum(c + 1, last), 1)
        score_keys(c, 0)
        idx_dots(jnp.minimum(c + 2, last), 0)
        score_keys(jnp.minimum(c + 1, last), 1)
        return carry

    lax.fori_loop(0, (n_ch + 1) // 2, score_pair, 0)

    n_pairs = (n_ch + 1) // 2

    @pl.when(n_ch % 2 == 1)
    def _():
        key_sc[pl.ds(pl.multiple_of(n_ch * CK, CK), CK), :] = jnp.full((CK, tq), KEY_NEG_INF, jnp.int32)

    row_iota2 = lax.broadcasted_iota(jnp.int32, (2 * CK, tq), 0)

    def count_rows(pred):
        def body(c2, acc):
            r0 = pl.multiple_of(c2 * 2 * CK, 2 * CK)
            hit = jnp.where(pred(key_sc[pl.ds(r0, 2 * CK), :], r0 + row_iota2), 1, 0)
            return acc + jnp.sum(hit.reshape(2 * CK // SUBLANES, SUBLANES, tq), axis=0)
        acc = lax.fori_loop(0, n_pairs, body, jnp.zeros((SUBLANES, tq), jnp.int32))
        return jnp.sum(acc, axis=0, keepdims=True)

    thr, cut = _topk_threshold(count_rows, n_pairs * 2 * CK, topk, key_sc.shape[0], cut_sc)

    def bias_chunk(c, carry):
        r0 = pl.multiple_of(c * CK, CK)
        key = key_sc[pl.ds(r0, CK), :]
        rows = r0 + row_iota
        sel = (rows <= t_idx) & ((key > thr) | ((key == thr) & (rows <= cut)))
        bias_sc[pl.ds(r0, CK), :] = jnp.where(sel, 0.0, MASK_BIAS)
        return carry

    lax.fori_loop(0, n_ch, bias_chunk, 0)

    def tiled_bias(r0):
        return jnp.concatenate([bias_sc[pl.ds(r0, CK), :]] * n_rep, axis=1)

    def scores(g, r0):
        return jnp.dot(k_ref[g, pl.ds(r0, CK), :], q_ref[g], preferred_element_type=F32)

    l_sc[...] = jnp.zeros(l_sc.shape, F32)
    acc_sc[...] = jnp.zeros(acc_sc.shape, F32)
    shift = []
    for g in range(n_groups):
        qf = q_ref[g].astype(F32)
        q_norm = jnp.sqrt(jnp.sum(qf * qf, axis=0, keepdims=True))
        shift.append(q_norm * jnp.concatenate([kmax_ref[g]] * n_rep, axis=1))

    def stage_scores(c, slot):
        cc = jnp.minimum(c, last)
        r0 = pl.multiple_of(cc * CK, CK)
        bias = jnp.where(c <= last, tiled_bias(r0), MASK_BIAS)
        for g in range(n_groups):
            s_sc[slot, g] = scores(g, r0) + bias - shift[g]

    def accumulate(c, slot):
        cc = jnp.minimum(c, last)
        for g in range(n_groups):
            p = jnp.exp(s_sc[slot, g])
            l_sc[g] = l_sc[g] + jnp.sum(p, axis=0, keepdims=True)
            acc_sc[g] = acc_sc[g] + jnp.dot(vt_ref[g, cc], p.astype(BF16), preferred_element_type=F32)

    stage_scores(0, 0)

    def bound_pair(c2, carry):
        c = 2 * c2
        stage_scores(c + 1, 1)
        accumulate(c, 0)
        stage_scores(c + 2, 0)
        accumulate(c + 1, 1)
        return carry

    lax.fori_loop(0, (n_ch + 1) // 2, bound_pair, 0)

    @pl.when(jnp.min(l_sc[...]) < MIN_TRUSTED_SUM)
    def _():
        m_sc[...] = jnp.full(m_sc.shape, MASK_BIAS, F32)
        l_sc[...] = jnp.zeros(l_sc.shape, F32)
        acc_sc[...] = jnp.zeros(acc_sc.shape, F32)

        def online_chunk(c, carry):
            r0 = pl.multiple_of(c * CK, CK)
            bias = tiled_bias(r0)
            for g in range(n_groups):
                s = scores(g, r0) + bias
                m_old = m_sc[g]
                m_new = jnp.maximum(m_old, jnp.max(s, axis=0, keepdims=True))
                alpha = jnp.exp(m_old - m_new)
                p = jnp.exp(s - m_new)
                l_sc[g] = alpha * l_sc[g] + jnp.sum(p, axis=0, keepdims=True)
                pv = jnp.dot(vt_ref[g, c], p.astype(BF16), preferred_element_type=F32)
                acc_sc[g] = alpha * acc_sc[g] + pv
                m_sc[g] = m_new
            return carry

        lax.fori_loop(0, n_ch, online_chunk, 0)

    heads = []
    for g in range(n_groups):
        og = acc_sc[g] / l_sc[g]
        heads += [og[:, r * tq:(r + 1) * tq] for r in range(n_rep)]
    o_ref[...] = jnp.concatenate(heads, axis=0).T.astype(o_ref.dtype)


def dsa_attention_prompt(q, k, v, qi, ki, wi, topk):
    B, T, H, hd = q.shape
    G = k.shape[2]
    R = H // G
    IH, ID = qi.shape[2], qi.shape[3]
    nb = T // TQ
    assert T % (2 * CK) == 0 and T % TQ == 0, "key chunks are walked in pairs"
    qi_l = qi.astype(BF16).reshape(B, T, IH * ID)
    q_l = (q * (hd ** -0.5)).astype(BF16).reshape(B, T, H * hd)
    w_l = wi.astype(F32).reshape(B, nb, TQ, IH).transpose(0, 1, 3, 2)
    k_l = k.astype(BF16).transpose(0, 2, 1, 3)
    vt_l = v.astype(BF16).reshape(B, T // CK, CK, G, hd).transpose(0, 3, 1, 4, 2)
    ki_l = ki.astype(BF16)
    kmax = jnp.sqrt(jnp.max(jnp.sum(k_l.astype(F32) ** 2, axis=-1), axis=-1))
    kmax = jnp.broadcast_to(kmax[:, :, None, None], (B, G, 1, TQ))
    kern = functools.partial(_dsa_attn_kernel, topk=topk, n_idx_heads=IH, n_groups=G, n_rep=R)
    return pl.pallas_call(
        kern,
        grid=(B, nb),
        in_specs=[
            pl.BlockSpec((None, TQ, IH * ID), lambda b, i: (b, i, 0)),
            pl.BlockSpec((None, None, IH, TQ), lambda b, i: (b, i, 0, 0)),
            pl.BlockSpec((None, TQ, H * hd), lambda b, i: (b, i, 0)),
            pl.BlockSpec((None, T, ID), lambda b, i: (b, 0, 0)),
            pl.BlockSpec((None, G, T, hd), lambda b, i: (b, 0, 0, 0)),
            pl.BlockSpec((None, G, T // CK, hd, CK), lambda b, i: (b, 0, 0, 0, 0)),
            pl.BlockSpec((None, G, 1, TQ), lambda b, i: (b, 0, 0, 0)),
        ],
        out_specs=pl.BlockSpec((None, TQ, H * hd), lambda b, i: (b, i, 0)),
        out_shape=jax.ShapeDtypeStruct((B, T, H * hd), BF16),
        scratch_shapes=[
            pltpu.VMEM((T, TQ), jnp.int32),
            pltpu.VMEM((T, TQ), F32),
            pltpu.VMEM((1, TQ), jnp.int32),
            pltpu.VMEM((G, 1, R * TQ), F32),
            pltpu.VMEM((G, 1, R * TQ), F32),
            pltpu.VMEM((G, hd, R * TQ), F32),
            pltpu.VMEM((2, CK, IH * TQ), F32),
            pltpu.VMEM((2, G, CK, R * TQ), F32),
            pltpu.VMEM((ID, IH * TQ), BF16),
            pltpu.VMEM((G, hd, R * TQ), BF16),
        ],
        compiler_params=_params("arbitrary", "arbitrary"),
        name="dsa_attention_prompt",
    )(qi_l, w_l, q_l, ki_l, k_l, vt_l, kmax)


def _final_norm_kernel(x_ref, g_ref, o_ref):
    o_ref[...] = _rms(x_ref[...], g_ref[...])


def final_rmsnorm(x, g, *, tm=512):
    M, K = x.shape
    tm = min(tm, M)
    return pl.pallas_call(
        _final_norm_kernel,
        grid=(M // tm,),
        in_specs=[pl.BlockSpec((tm, K), lambda i: (i, 0)), pl.BlockSpec((1, K), lambda i: (0, 0))],
        out_specs=pl.BlockSpec((tm, K), lambda i: (i, 0)),
        out_shape=jax.ShapeDtypeStruct((M, K), F32),
        compiler_params=_params("arbitrary"),
        name="final_rmsnorm",
    )(x, g.reshape(1, K).astype(F32))


def rope_partial(x, pos):
    d_rot = x.shape[-1] // ROPE_FRAC
    half = d_rot // 2
    inv = ROPE_THETA ** (-jnp.arange(half, dtype=F32) / half)
    ang = pos.astype(F32)[:, None] * inv[None, :]
    cos = jnp.cos(ang)[:, None, :]
    sin = jnp.sin(ang)[:, None, :]
    xr = x[..., :d_rot].astype(F32)
    x1, x2 = xr[..., :half], xr[..., half:]
    rot = jnp.concatenate([x1 * cos - x2 * sin, x2 * cos + x1 * sin], axis=-1)
    return jnp.concatenate([rot.astype(x.dtype), x[..., d_rot:]], axis=-1)


def _pad_cols(w, n):
    return jnp.pad(w, ((0, 0), (0, n - w.shape[1])))


def gla_layer(x, B, T, norm, w_in, w_a2, b_a, g_out, w_out, s0, hp):
    wdt = F32 if hp else BF16
    HK, HV = GLA_HEADS * GLA_DK, GLA_HEADS * GLA_DV
    n_pad = _round_up(w_in.shape[1], LANES)
    segs = [(0, HK), (HK, HK), (2 * HK, HV), (2 * HK + HV, HV), (2 * HK + 2 * HV, n_pad - 2 * HK - 2 * HV)]
    w2 = jnp.pad(w_a2, ((0, segs[-1][1] - GLA_RANK), (0, 0))).astype(wdt)
    q, k, v, r, g = norm_linear(x, norm, _pad_cols(w_in, n_pad).astype(wdt), segs=segs, tail=(w2, b_a),
                                hp=hp, name="gla_project")
    if s0 is None:
        sh = lambda a: a.reshape(B, T, a.shape[1])
        og, s_new = gla_scan(sh(q), sh(k), sh(g), sh(v), sh(r), g_out)
        og = og.reshape(B * T, HV)
    else:
        og, s_new = gla_step(q, k, g, v, r, g_out, s0)
    (x,) = norm_linear(og, None, w_out.astype(wdt), res=x, hp=hp, tm=512, name="gla_out_project")
    return x, s_new


def dsa_project(x, B, T, norm, w_in, pos, hp):
    wdt = F32 if hp else BF16
    QW, KW, IW = ATT_HEADS * HEAD_DIM, KV_HEADS * HEAD_DIM, IDX_HEADS * IDX_DIM
    n_pad = _round_up(w_in.shape[1], LANES)
    off_i = QW + 2 * KW + IW
    segs = [(0, QW), (QW, KW), (QW + KW, KW), (QW + 2 * KW, IW), (off_i, n_pad - off_i)]
    q, k, v, qi, kw = norm_linear(x, norm, _pad_cols(w_in, n_pad).astype(wdt), segs=segs, hp=hp, name="dsa_project")
    ki, wi = kw[:, :IDX_DIM], kw[:, IDX_DIM:IDX_DIM + IDX_HEADS]
    q = rope_partial(q.reshape(B, T, ATT_HEADS, HEAD_DIM), pos)
    k = rope_partial(k.reshape(B, T, KV_HEADS, HEAD_DIM), pos)
    v = v.reshape(B, T, KV_HEADS, HEAD_DIM)
    qi = rope_partial(qi.reshape(B, T, IDX_HEADS, IDX_DIM), pos)
    ki = rope_partial(ki.reshape(B, T, 1, IDX_DIM), pos)[:, :, 0]
    wi = wi.reshape(B, T, IDX_HEADS) * (IDX_HEADS ** -0.5 * IDX_DIM ** -0.5)
    return q, k, v, qi, ki, wi


def dsa_prompt_layer(x, B, T, norm, w_in, w_out):
    q, k, v, qi, ki, wi = dsa_project(x, B, T, norm, w_in, jnp.arange(T), False)
    o = dsa_attention_prompt(q, k, v, qi, ki, wi, min(TOPK_MAX, T // 4))
    (x,) = norm_linear(o.reshape(B * T, -1), None, w_out.astype(BF16), res=x, tm=512, name="dsa_out_project")
    return x, k, v, ki


def dsa_sample_layer(x, norm, w_in, w_out, cache_k, cache_v, cache_ki, page_table):
    DB, Tn = x.shape[0], 1
    pos = PAST_LEN + jnp.arange(Tn)
    q, k, v, qi, ki, wi = dsa_project(x, DB, Tn, norm, w_in, pos, True)
    L = PAST_LEN + Tn
    topk = min(TOPK_MAX, L // 4)
    with jax.default_matmul_precision("highest"):
        ki_past = cache_ki[page_table].reshape(DB, PAST_LEN, IDX_DIM)
        ki_all = jnp.concatenate([ki_past, ki], axis=1)
        dots = jnp.einsum('bthd,bsd->bths', qi, ki_all)
        sc = jnp.einsum('bth,bths->bts', wi, jax.nn.relu(dots))
        sc = jnp.where(jnp.arange(L)[None, None, :] <= pos[None, :, None], sc, -jnp.inf)
        idx = topk_indices(sc.reshape(DB * Tn, L), topk).reshape(DB, Tn, topk)
        valid = idx <= pos[None, :, None]
        in_past = idx < PAST_LEN
        pidx = jnp.minimum(idx, PAST_LEN - 1)
        phys = jax.vmap(lambda pt, ii: pt[ii])(page_table, pidx // PAGE_SIZE)
        row = phys * PAGE_SIZE + pidx % PAGE_SIZE
        nidx = jnp.clip(idx - PAST_LEN, 0, Tn - 1)
        sel = in_past[..., None, None]
        gather_rows = jax.vmap(lambda aa, ii: aa[ii])
        cached = lambda c: jnp.take(c.reshape(-1, KV_HEADS, HEAD_DIM), row, axis=0)
        kg = jnp.where(sel, cached(cache_k), gather_rows(k, nidx))
        vg = jnp.where(sel, cached(cache_v), gather_rows(v, nidx))
        qg = q.reshape(DB, Tn, KV_HEADS, ATT_HEADS // KV_HEADS, HEAD_DIM)
        s = jnp.einsum('btgrd,btkgd->btgrk', qg, kg) * (HEAD_DIM ** -0.5)
        s = jnp.where(valid[:, :, None, None, :], s, -jnp.inf)
        pr = jax.nn.softmax(s, axis=-1)
        o = jnp.einsum('btgrk,btkgd->btgrd', pr, vg).reshape(DB * Tn, ATT_HEADS * HEAD_DIM)
    (x,) = norm_linear(o, None, w_out, res=x, hp=True, name="dsa_out_project_s")
    return x, k, v, ki


def kernel(x_prompt, x_sample, state_gla, cache_k, cache_v, cache_idx_k, page_table,
           gla_norm, gla_w_in, gla_w_a2, gla_b_a, gla_out_norm, gla_w_out,
           dense_norm, dense_w_gate, dense_w_up, dense_w_down,
           dsa_norm, dsa_w_in, dsa_w_out,
           moe_norm, moe_w_router, moe_w_gate, moe_w_up, moe_w_down,
           final_norm):
    B, T, D = x_prompt.shape
    DB = x_sample.shape[0]
    xp = x_prompt.reshape(B * T, D)
    xs = x_sample.reshape(DB, D)
    sg_p, sg_s = [], []
    kp, vp, ip, ks_, vs_, is_ = [], [], [], [], [], []
    n_layers = gla_norm.shape[0] + dsa_norm.shape[0]
    prompt_normed = False
    for i in range(n_layers):
        j = i // 2
        if i % 2 == 0:
            xp, sp = gla_layer(xp, B, T, gla_norm[j], gla_w_in[j], gla_w_a2[j], gla_b_a[j], gla_out_norm[j],
                               gla_w_out[j], None, False)
            xs, ss = gla_layer(xs, DB, 1, gla_norm[j], gla_w_in[j], gla_w_a2[j], gla_b_a[j], gla_out_norm[j],
                               gla_w_out[j], state_gla[j], True)
            sg_p.append(sp)
            sg_s.append(ss)
            wg, wu, wd = dense_w_gate[j][None], dense_w_up[j][None], dense_w_down[j][None]
            xp = ffn(xp, dense_norm[j], wg.astype(BF16), wu.astype(BF16), wd.astype(BF16), name="dense_ffn")
            xs = ffn(xs, dense_norm[j], wg, wu, wd, hp=True, name="dense_ffn_s")
        else:
            xp, kpn, vpn, ipn = dsa_prompt_layer(xp, B, T, dsa_norm[j], dsa_w_in[j], dsa_w_out[j])
            xs, ksn, vsn, isn = dsa_sample_layer(xs, dsa_norm[j], dsa_w_in[j], dsa_w_out[j],
                                                 cache_k[j], cache_v[j], cache_idx_k[j], page_table)
            kp.append(kpn); vp.append(vpn); ip.append(ipn)
            ks_.append(ksn); vs_.append(vsn); is_.append(isn)
            wg, wu, wd = moe_w_gate[j], moe_w_up[j], moe_w_down[j]
            prompt_normed = i == n_layers - 1
            xp = moe_routed(xp, moe_norm[j], wg.astype(BF16), wu.astype(BF16), wd.astype(BF16),
                            router_gates(xp, moe_norm[j], moe_w_router[j]), final_norm if prompt_normed else None)
            xs = ffn(xs, moe_norm[j], wg, wu, wd, router_gates(xs, moe_norm[j], moe_w_router[j]),
                     hp=True, name="moe_ffn_s")
    y_prompt = (xp if prompt_normed else final_rmsnorm(xp, final_norm)).reshape(B, T, D)
    y_sample = final_rmsnorm(xs, final_norm).reshape(DB, 1, D)
    return (y_prompt, y_sample, jnp.stack(sg_p), jnp.stack(sg_s),
            jnp.stack(kp), jnp.stack(vp), jnp.stack(ip),
            jnp.stack(ks_), jnp.stack(vs_), jnp.stack(is_))
```

```python
import functools

import jax, jax.numpy as jnp
from jax import lax
from jax.experimental import pallas as pl
from jax.experimental.pallas import tpu as pltpu

D_MODEL = 1024
PAST_LEN = 16384
PAGE_SIZE = 128
GLA_HEADS = 4
GLA_DK = D_MODEL // 2 // GLA_HEADS
GLA_DV = D_MODEL // GLA_HEADS
GLA_RANK = 16
GLA_TAU = 16.0
GLA_CHUNK = 32
ATT_HEADS = 16
KV_HEADS = 4
HEAD_DIM = D_MODEL // ATT_HEADS
IDX_HEADS = 8
IDX_DIM = 64
TOPK_MAX = 256
ROPE_THETA = 500000.0
ROPE_FRAC = 4
D_FF = 2816
N_EXPERTS = 8
EPS = 1e-6

LANES = 128
SUBLANES = 8
BF16_ROWS = 16
LOG2E = 1.4426950408889634
TQ = 128
CK = 256
MASK_BIAS = -1e30
MIN_TRUSTED_SUM = 1e-25
INT32_MIN = -2 ** 31
INT32_MAX = 2 ** 31 - 1
KEY_NEG_INF = -2139095041
VMEM_LIMIT_BYTES = 52 * 1024 * 1024
F32 = jnp.float32
BF16 = jnp.bfloat16


def _round_up(n, m):
    return -(-n // m) * m


def _params(*sem):
    return pltpu.CompilerParams(dimension_semantics=sem, vmem_limit_bytes=VMEM_LIMIT_BYTES)


def _mxu_dot(a, b, hp):
    if hp:
        return jnp.dot(a.astype(F32), b.astype(F32), preferred_element_type=F32, precision=lax.Precision.HIGHEST)
    return jnp.dot(a.astype(BF16), b.astype(BF16), preferred_element_type=F32)


def _rms(x, g):
    return x * lax.rsqrt(jnp.mean(x * x, axis=-1, keepdims=True) + EPS) * g


def _norm_linear_kernel(*refs, apply_norm, has_res, has_tail, segs, hp):
    it = iter(refs)
    x_ref = next(it)
    g_ref = next(it) if apply_norm else None
    w_ref = next(it)
    res_ref = next(it) if has_res else None
    w2_ref, b2_ref = (next(it), next(it)) if has_tail else (None, None)
    out_refs = list(it)
    x = x_ref[...]
    if apply_norm:
        x = _rms(x.astype(F32), g_ref[...])
    y = _mxu_dot(x, w_ref[...], hp)
    if has_res:
        y = y + res_ref[...]
    for n, (o_ref, (off, width)) in enumerate(zip(out_refs, segs)):
        seg = y[:, off:off + width]
        if has_tail and n == len(segs) - 1:
            z = _mxu_dot(seg, w2_ref[...], hp) + b2_ref[...]
            seg = (jnp.minimum(z, 0.0) - jnp.log(1.0 + jnp.exp(-jnp.abs(z)))) * (1.0 / GLA_TAU)
        o_ref[...] = seg.astype(o_ref.dtype)


def norm_linear(x, g, w, *, res=None, segs=None, out_dtypes=None, tail=None, hp=False, tm=256, name="norm_linear"):
    M, K = x.shape
    N = w.shape[1]
    segs = segs or [(0, N)]
    out_dtypes = out_dtypes or [F32] * len(segs)
    tm = min(tm, M)
    args, in_specs = [x], [pl.BlockSpec((tm, K), lambda i: (i, 0))]
    if g is not None:
        args.append(g.reshape(1, K).astype(F32))
        in_specs.append(pl.BlockSpec((1, K), lambda i: (0, 0)))
    args.append(w)
    in_specs.append(pl.BlockSpec((K, N), lambda i: (0, 0)))
    if res is not None:
        args.append(res)
        in_specs.append(pl.BlockSpec((tm, N), lambda i: (i, 0)))
    out_widths = [wd for _, wd in segs]
    if tail is not None:
        w2, b2 = tail
        args += [w2, b2.reshape(1, -1).astype(F32)]
        in_specs += [pl.BlockSpec(w2.shape, lambda i: (0, 0)), pl.BlockSpec((1, w2.shape[1]), lambda i: (0, 0))]
        out_widths[-1] = w2.shape[1]
    kern = functools.partial(_norm_linear_kernel, apply_norm=g is not None, has_res=res is not None,
                             has_tail=tail is not None, segs=tuple(segs), hp=hp)
    outs = pl.pallas_call(
        kern,
        grid=(M // tm,),
        in_specs=in_specs,
        out_specs=[pl.BlockSpec((tm, wd), lambda i: (i, 0)) for wd in out_widths],
        out_shape=[jax.ShapeDtypeStruct((M, wd), dt) for wd, dt in zip(out_widths, out_dtypes)],
        compiler_params=_params("arbitrary"),
        name=name,
    )(*args)
    return outs


def _ffn_kernel(*refs, has_gate, hp):
    if has_gate:
        x_ref, g_ref, gate_ref, wg_ref, wu_ref, wd_ref, o_ref, h_sc = refs
    else:
        x_ref, g_ref, wg_ref, wu_ref, wd_ref, o_ref, h_sc = refs
        gate_ref = None
    e = pl.program_id(1)
    f = pl.program_id(2)

    @pl.when((e == 0) & (f == 0))
    def _():
        x = x_ref[...]
        h_sc[...] = _rms(x, g_ref[...]).astype(h_sc.dtype)
        o_ref[...] = x

    h = h_sc[...]
    a = _mxu_dot(h, wg_ref[...], hp)
    u = _mxu_dot(h, wu_ref[...], hp)
    y = _mxu_dot(a * jax.nn.sigmoid(a) * u, wd_ref[...], hp)
    if has_gate:
        gate = gate_ref[...]
        lane = lax.broadcasted_iota(jnp.int32, gate.shape, 1)
        y = y * jnp.sum(jnp.where(lane == e, gate, 0.0), axis=-1, keepdims=True)
    o_ref[...] += y


def ffn(x, g, w_gate, w_up, w_down, gates=None, *, hp=False, tm=512, tf=1408, name="ffn"):
    M, K = x.shape
    E, _, F = w_gate.shape
    tm = min(tm, M)
    args = [x, g.reshape(1, K).astype(F32)]
    in_specs = [pl.BlockSpec((tm, K), lambda i, e, f: (i, 0)), pl.BlockSpec((1, K), lambda i, e, f: (0, 0))]
    if gates is not None:
        args.append(gates)
        in_specs.append(pl.BlockSpec((tm, gates.shape[1]), lambda i, e, f: (i, 0)))
    args += [w_gate, w_up, w_down]
    in_specs += [pl.BlockSpec((None, K, tf), lambda i, e, f: (e, 0, f)),
                 pl.BlockSpec((None, K, tf), lambda i, e, f: (e, 0, f)),
                 pl.BlockSpec((None, tf, K), lambda i, e, f: (e, f, 0))]
    return pl.pallas_call(
        functools.partial(_ffn_kernel, has_gate=gates is not None, hp=hp),
        grid=(M // tm, E, F // tf),
        in_specs=in_specs,
        out_specs=pl.BlockSpec((tm, K), lambda i, e, f: (i, 0)),
        out_shape=jax.ShapeDtypeStruct((M, K), F32),
        scratch_shapes=[pltpu.VMEM((tm, K), F32 if hp else BF16)],
        compiler_params=_params("arbitrary", "arbitrary", "arbitrary"),
        name=name,
    )(*args)


def _router_kernel(x_ref, g_ref, w_ref, o_ref, *, n_experts):
    h = _rms(x_ref[...], g_ref[...])
    logits = _mxu_dot(h, w_ref[...], True)
    lane = lax.broadcasted_iota(jnp.int32, logits.shape, 1)
    logits = jnp.where(lane < n_experts, logits, -jnp.inf)
    m1 = jnp.max(logits, axis=-1, keepdims=True)
    i1 = jnp.min(jnp.where(logits == m1, lane, LANES), axis=-1, keepdims=True)
    rest = jnp.where(lane == i1, -jnp.inf, logits)
    m2 = jnp.max(rest, axis=-1, keepdims=True)
    i2 = jnp.min(jnp.where(rest == m2, lane, LANES), axis=-1, keepdims=True)
    e2 = jnp.exp(m2 - m1)
    denom = 1.0 + e2
    o_ref[...] = jnp.where(lane == i1, 1.0 / denom, 0.0) + jnp.where(lane == i2, e2 / denom, 0.0)


def router_gates(x, g, w_router, *, tm=512):
    M, K = x.shape
    E = w_router.shape[1]
    tm = min(tm, M)
    w = jnp.pad(w_router.astype(F32), ((0, 0), (0, LANES - E)))
    return pl.pallas_call(
        functools.partial(_router_kernel, n_experts=E),
        grid=(M // tm,),
        in_specs=[pl.BlockSpec((tm, K), lambda i: (i, 0)), pl.BlockSpec((1, K), lambda i: (0, 0)),
                  pl.BlockSpec((K, LANES), lambda i: (0, 0))],
        out_specs=pl.BlockSpec((tm, LANES), lambda i: (i, 0)),
        out_shape=jax.ShapeDtypeStruct((M, LANES), F32),
        compiler_params=_params("arbitrary"),
        name="moe_router",
    )(x, g.reshape(1, K).astype(F32), w)


MOE_TB = 1024
MOE_SUB = 288


def _moe_routed_kernel(nsub_ref, x_ref, g_ref, gout_ref, rrow_ref, rcol_ref, gcol_ref, wg_ref, wu_ref, wd_ref, o_ref,
                       h_sc, xg_sc, y_sc, *, n_experts, n_f, norm_out):
    i, e, f = pl.program_id(0), pl.program_id(1), pl.program_id(2)
    tb = x_ref.shape[0]
    n_tiles = tb // LANES

    @pl.when((e == 0) & (f == 0))
    def _():
        x = x_ref[...]
        h_sc[...] = _rms(x, g_ref[...]).astype(h_sc.dtype)
        o_ref[...] = x

    sub_iota = lax.broadcasted_iota(jnp.int32, (MOE_SUB, tb), 0)
    lane_iota = lax.broadcasted_iota(jnp.int32, (LANES, MOE_SUB), 1)

    def sub_body(s, carry):
        base = s * MOE_SUB

        @pl.when(f == 0)
        def _():
            onehot = jnp.where(rrow_ref[...] == base + sub_iota, 1.0, 0.0).astype(BF16)
            xg_sc[s] = jnp.dot(onehot, h_sc[...], preferred_element_type=F32).astype(xg_sc.dtype)

        xg = xg_sc[s]
        a = _mxu_dot(xg, wg_ref[...], False)
        u = _mxu_dot(xg, wu_ref[...], False)
        y = _mxu_dot(a * jax.nn.sigmoid(a) * u, wd_ref[...], False)

        @pl.when(f == 0)
        def _():
            y_sc[s] = y

        if n_f > 2:
            @pl.when((f > 0) & (f < n_f - 1))
            def _():
                y_sc[s] += y

        @pl.when(f == n_f - 1)
        def _():
            y_all = (y if n_f == 1 else y_sc[s] + y).astype(BF16)
            onehot_t = jnp.concatenate(
                [jnp.where(rcol_ref[:, t:t + 1] == base + lane_iota, 1.0, 0.0).astype(BF16)
                 for t in range(n_tiles)], axis=0)
            back = jnp.dot(onehot_t, y_all, preferred_element_type=F32)
            for t in range(n_tiles):
                rows = slice(t * LANES, (t + 1) * LANES)
                o_ref[rows, :] += gcol_ref[:, t:t + 1] * back[rows, :]
        return carry

    lax.fori_loop(0, nsub_ref[i * n_experts + e], sub_body, 0)

    if norm_out:
        @pl.when((e == n_experts - 1) & (f == n_f - 1))
        def _():
            o_ref[...] = _rms(o_ref[...], gout_ref[...])


def moe_routed(x, g, w_gate, w_up, w_down, gates, out_gain=None, *, tf=1408):
    M, K = x.shape
    E, _, F = w_gate.shape
    tb = min(MOE_TB, M)
    nb = M // tb
    g8 = gates[:, :E].reshape(nb, tb, E)
    routed = g8 != 0.0
    rank = jnp.where(routed, jnp.cumsum(routed.astype(jnp.int32), axis=1) - 1, -1).transpose(0, 2, 1)
    n_sub = (jnp.sum(routed, axis=1, dtype=jnp.int32) + MOE_SUB - 1) // MOE_SUB
    to_col = lambda a: a.reshape(nb, E, tb // LANES, LANES).transpose(0, 1, 3, 2)
    c_spec = pl.BlockSpec((None, None, LANES, tb // LANES), lambda i, e, f, ns: (i, e, 0, 0))
    grid_spec = pltpu.PrefetchScalarGridSpec(
        num_scalar_prefetch=1,
        grid=(nb, E, F // tf),
        in_specs=[pl.BlockSpec((tb, K), lambda i, e, f, ns: (i, 0)),
                  pl.BlockSpec((1, K), lambda i, e, f, ns: (0, 0)),
                  pl.BlockSpec((1, K), lambda i, e, f, ns: (0, 0)),
                  pl.BlockSpec((None, None, 1, tb), lambda i, e, f, ns: (i, e, 0, 0)),
                  c_spec, c_spec,
                  pl.BlockSpec((None, K, tf), lambda i, e, f, ns: (e, 0, f)),
                  pl.BlockSpec((None, K, tf), lambda i, e, f, ns: (e, 0, f)),
                  pl.BlockSpec((None, tf, K), lambda i, e, f, ns: (e, f, 0))],
        out_specs=pl.BlockSpec((tb, K), lambda i, e, f, ns: (i, 0)),
        scratch_shapes=[pltpu.VMEM((tb, K), BF16), pltpu.VMEM((-(-tb // MOE_SUB), MOE_SUB, K), BF16),
                        pltpu.VMEM((-(-tb // MOE_SUB), MOE_SUB, K), F32)],
    )
    return pl.pallas_call(
        functools.partial(_moe_routed_kernel, n_experts=E, n_f=F // tf, norm_out=out_gain is not None),
        grid_spec=grid_spec,
        out_shape=jax.ShapeDtypeStruct((M, K), F32),
        compiler_params=_params("arbitrary", "arbitrary", "arbitrary"),
        name="moe_routed",
    )(n_sub.reshape(-1), x, g.reshape(1, K).astype(F32),
      (jnp.ones((K,), F32) if out_gain is None else out_gain).reshape(1, K).astype(F32), rank[:, :, None, :], to_col(rank),
      to_col(g8.transpose(0, 2, 1)), w_gate, w_up, w_down)


def _gla_scan_kernel(q_ref, k_ref, g_ref, v_ref, r_ref, gout_ref, og_ref, s_ref, st_sc, *, chunk):
    t = pl.program_id(2)
    tt, dk = q_ref.shape
    n_tiles = chunk // SUBLANES

    @pl.when(t == 0)
    def _():
        st_sc[...] = jnp.zeros(st_sc.shape, F32)

    tri = (lax.broadcasted_iota(jnp.int32, (chunk, chunk), 0)
           >= lax.broadcasted_iota(jnp.int32, (chunk, chunk), 1)).astype(F32)
    row8 = lax.broadcasted_iota(jnp.int32, (SUBLANES, dk), 0)

    def chunk_body(c, carry):
        r0 = pl.multiple_of(c * chunk, chunk)
        qc = q_ref[pl.ds(r0, chunk), :] * (dk ** -0.5)
        kc = k_ref[pl.ds(r0, chunk), :]
        vc = v_ref[pl.ds(r0, chunk), :]
        b = jnp.dot(tri, g_ref[pl.ds(r0, chunk), :], preferred_element_type=F32,
                    precision=lax.Precision.HIGHEST)
        b_last = b[chunk - 1:chunk, :]
        st = st_sc[...]
        o_inter = lax.dot_general((qc * jnp.exp(b)).astype(BF16), st.astype(BF16),
                                  (((1,), (1,)), ((), ())), preferred_element_type=F32)
        o_t = [o_inter[i * SUBLANES:(i + 1) * SUBLANES, :] for i in range(n_tiles)]
        q_t = [qc[i * SUBLANES:(i + 1) * SUBLANES, :] for i in range(n_tiles)]
        b_t = [b[i * SUBLANES:(i + 1) * SUBLANES, :] for i in range(n_tiles)]
        for s in range(chunk):
            k_s, b_s, v_s = kc[s:s + 1, :], b[s:s + 1, :], vc[s:s + 1, :]
            for i in range(s // SUBLANES, n_tiles):
                diff = b_t[i] - b_s
                if i == s // SUBLANES:
                    diff = jnp.where(row8 + i * SUBLANES >= s, diff, -jnp.inf)
                att = jnp.sum(q_t[i] * k_s * jnp.exp(diff), axis=-1, keepdims=True)
                o_t[i] = o_t[i] + att * v_s
        o = jnp.concatenate(o_t, axis=0)
        kd = kc * jnp.exp(b_last - b)
        ds = lax.dot_general(vc.astype(BF16), kd.astype(BF16), (((0,), (0,)), ((), ())),
                             preferred_element_type=F32)
        st_sc[...] = jnp.exp(b_last) * st + ds
        on = _rms(o, gout_ref[...])
        rc = r_ref[pl.ds(r0, chunk), :]
        og_ref[pl.ds(r0, chunk), :] = (on * (rc * jax.nn.sigmoid(rc))).astype(og_ref.dtype)
        return carry

    lax.fori_loop(0, tt // chunk, chunk_body, 0, unroll=8)

    @pl.when(t == pl.num_programs(2) - 1)
    def _():
        s_ref[...] = st_sc[...].T


def gla_scan(q, k, g, v, r, g_out, *, tt=512):
    B, T, _ = q.shape
    H, DV = g_out.shape
    DK = q.shape[2] // H
    tt = min(tt, T)
    qk_spec = pl.BlockSpec((None, tt, DK), lambda b, h, t: (b, t, h))
    v_spec = pl.BlockSpec((None, tt, DV), lambda b, h, t: (b, t, h))
    return pl.pallas_call(
        functools.partial(_gla_scan_kernel, chunk=GLA_CHUNK),
        grid=(B, H, T // tt),
        in_specs=[qk_spec, qk_spec, qk_spec, v_spec, v_spec, pl.BlockSpec((None, 1, DV), lambda b, h, t: (h, 0, 0))],
        out_specs=[v_spec, pl.BlockSpec((None, None, DK, DV), lambda b, h, t: (b, h, 0, 0))],
        out_shape=[jax.ShapeDtypeStruct((B, T, H * DV), BF16), jax.ShapeDtypeStruct((B, H, DK, DV), F32)],
        scratch_shapes=[pltpu.VMEM((DV, DK), F32)],
        compiler_params=_params("arbitrary", "arbitrary", "arbitrary"),
        name="gla_scan",
    )(q, k, g, v, r, g_out.reshape(H, 1, DV).astype(F32))


def _gla_step_kernel(qa_ref, kcol_ref, acol_ref, qk_ref, v_ref, r_ref, gout_ref, s_ref, og_ref, sn_ref):
    s = s_ref[...]
    v = v_ref[...]
    o = jnp.sum(qa_ref[...] * s, axis=1, keepdims=True) + qk_ref[...] * v
    sn_ref[...] = acol_ref[...] * s + kcol_ref[...] * v
    on = _rms(o, gout_ref[...])
    r = r_ref[...]
    og_ref[...] = on * (r * jax.nn.sigmoid(r))


def gla_step(q, k, g, v, r, g_out, s0):
    B = q.shape[0]
    H, DV = g_out.shape
    DK = q.shape[1] // H
    qh = q.reshape(B, H, DK) * (DK ** -0.5)
    kh = k.reshape(B, H, DK)
    ah = jnp.exp(g.reshape(B, H, DK))
    col = lambda a: a[..., None]
    qk = jnp.sum(qh * kh, axis=-1)[..., None, None]
    row = lambda a: a.reshape(B, H, 1, DV)
    c_spec = pl.BlockSpec((None, H, DK, 1), lambda b: (b, 0, 0, 0))
    r_spec = pl.BlockSpec((None, H, 1, DV), lambda b: (b, 0, 0, 0))
    s_spec = pl.BlockSpec((None, H, DK, DV), lambda b: (b, 0, 0, 0))
    og, sn = pl.pallas_call(
        _gla_step_kernel,
        grid=(B,),
        in_specs=[c_spec, c_spec, c_spec, pl.BlockSpec((None, H, 1, 1), lambda b: (b, 0, 0, 0)),
                  r_spec, r_spec, pl.BlockSpec((H, 1, DV), lambda b: (0, 0, 0)), s_spec],
        out_specs=[r_spec, s_spec],
        out_shape=[jax.ShapeDtypeStruct((B, H, 1, DV), F32), jax.ShapeDtypeStruct((B, H, DK, DV), F32)],
        compiler_params=_params("arbitrary"),
        name="gla_step",
    )(col(qh * ah), col(kh), col(ah), qk, row(v), row(r), g_out.reshape(H, 1, DV).astype(F32), s0)
    return og.reshape(B, H * DV), sn


def _score_keys(scores):
    bits = pltpu.bitcast(scores, jnp.int32)
    return jnp.where(bits < 0, bits ^ INT32_MAX, bits)


def _topk_threshold(count_rows, n_rows, topk, max_rows, cut_sc):
    lanes = cut_sc.shape[1]
    cnt_nonneg = count_rows(lambda key, rows: key >= 0)
    nonneg = cnt_nonneg >= topk
    thr0 = jnp.where(nonneg, 0, INT32_MIN)
    cnt0 = jnp.where(nonneg, cnt_nonneg, n_rows)

    def bit_step(b, carry):
        thr, cnt_thr = carry
        cand = thr | jnp.left_shift(jnp.int32(1), 30 - b)
        cnt = count_rows(lambda key, rows: key >= cand)
        ok = cnt >= topk
        return jnp.where(ok, cand, thr), jnp.where(ok, cnt, cnt_thr)

    thr, cnt_thr = lax.fori_loop(0, 31, bit_step, (thr0, cnt0))

    excess = (cnt_thr > topk) & (thr > KEY_NEG_INF)
    cut_sc[...] = jnp.full((1, lanes), INT32_MAX, jnp.int32)

    @pl.when(jnp.max(jnp.where(excess, 1, 0)) > 0)
    def _():
        need = topk - count_rows(lambda key, rows: key > thr)
        row_bits = (max_rows - 1).bit_length()

        def row_step(b, cut):
            cand = cut | jnp.left_shift(jnp.int32(1), row_bits - 1 - b)
            cnt = count_rows(lambda key, rows: (key == thr) & (rows < cand))
            return jnp.where(cnt < need, cand, cut)

        cut = lax.fori_loop(0, row_bits, row_step, jnp.zeros((1, lanes), jnp.int32))
        cut_sc[...] = jnp.where(excess, cut, INT32_MAX)

    return thr, cut_sc[...]


def _topk_mask_kernel(sc_ref, o_ref, key_sc, cut_sc, *, topk, n_valid, chunk):
    n_rows, lanes = sc_ref.shape
    n_chunks = n_rows // chunk
    row_iota = lax.broadcasted_iota(jnp.int32, (chunk, lanes), 0)

    def to_keys(c, carry):
        r0 = pl.multiple_of(c * chunk, chunk)
        key_sc[pl.ds(r0, chunk), :] = _score_keys(sc_ref[pl.ds(r0, chunk), :])
        return carry

    lax.fori_loop(0, n_chunks, to_keys, 0)

    def count_rows(pred):
        def body(c, acc):
            r0 = pl.multiple_of(c * chunk, chunk)
            hit = jnp.where(pred(key_sc[pl.ds(r0, chunk), :], r0 + row_iota), 1, 0)
            return acc + jnp.sum(hit.reshape(chunk // SUBLANES, SUBLANES, lanes), axis=0)
        acc = lax.fori_loop(0, n_chunks, body, jnp.zeros((SUBLANES, lanes), jnp.int32))
        return jnp.sum(acc, axis=0, keepdims=True)

    thr, cut = _topk_threshold(count_rows, n_rows, topk, n_rows, cut_sc)

    def write(c, carry):
        r0 = pl.multiple_of(c * chunk, chunk)
        key = key_sc[pl.ds(r0, chunk), :]
        rows = r0 + row_iota
        sel = (rows < n_valid) & ((key > thr) | ((key == thr) & (rows <= cut)))
        o_ref[pl.ds(r0, chunk), :] = jnp.where(sel, 1.0, 0.0)
        return carry

    lax.fori_loop(0, n_chunks, write, 0)


def topk_indices(scores, topk, *, chunk=512):
    Q, L = scores.shape
    rows = _round_up(L, chunk)
    sc = jnp.pad(scores.T, ((0, rows - L), (0, LANES - Q)), constant_values=-jnp.inf)
    mask = pl.pallas_call(
        functools.partial(_topk_mask_kernel, topk=topk, n_valid=L, chunk=chunk),
        out_shape=jax.ShapeDtypeStruct((rows, LANES), F32),
        scratch_shapes=[pltpu.VMEM((rows, LANES), jnp.int32), pltpu.VMEM((1, LANES), jnp.int32)],
        compiler_params=pltpu.CompilerParams(vmem_limit_bytes=VMEM_LIMIT_BYTES),
        name="topk_mask",
    )(sc)
    sel = mask[:L, :Q].T > 0.5
    return jax.vmap(lambda m: jnp.nonzero(m, size=topk, fill_value=0)[0])(sel).astype(jnp.int32)


def _dsa_attn_kernel(qi_in_ref, w_ref, q_in_ref, ki_ref, k_ref, vt_ref, kmax_ref, o_ref,
                     key_sc, bias_sc, cut_sc, m_sc, l_sc, acc_sc, d_sc, s_sc, qi_ref, q_ref,
                     *, topk, n_idx_heads, n_groups, n_rep, hd):
    tq = w_ref.shape[1]
    i = pl.program_id(1)

    def heads_to_lanes(src_ref, n_heads, dim, store):
        per_slab = LANES // dim
        for j in range(n_heads // per_slab):
            slab = src_ref[:, j * LANES:(j + 1) * LANES].astype(F32).T
            for u in range(per_slab):
                store(j * per_slab + u, slab[u * dim:(u + 1) * dim, :].astype(BF16))

    def store_qi(h, x):
        qi_ref[:, h * tq:(h + 1) * tq] = x

    def store_q(h, x):
        q_ref[h // n_rep, 0:hd, (h % n_rep) * tq:(h % n_rep + 1) * tq] = x

    heads_to_lanes(qi_in_ref, n_idx_heads, qi_ref.shape[0], store_qi)
    heads_to_lanes(q_in_ref, n_groups * n_rep, hd, store_q)
    q_ref[:, hd:, :] = jnp.zeros((n_groups, q_ref.shape[1] - hd, n_rep * tq), BF16)

    n_ch = (i * tq + tq + CK - 1) // CK
    t_idx = i * tq + lax.broadcasted_iota(jnp.int32, (CK, tq), 1)
    row_iota = lax.broadcasted_iota(jnp.int32, (CK, tq), 0)

    last = n_ch - 1

    def idx_dots(c, slot):
        r0 = pl.multiple_of(c * CK, CK)
        d_sc[slot] = jnp.dot(ki_ref[pl.ds(r0, CK), :], qi_ref[...], preferred_element_type=F32)

    def score_keys(c, slot):
        r0 = pl.multiple_of(c * CK, CK)
        d = d_sc[slot]
        acc = jnp.zeros((CK, tq), F32)
        for h in range(n_idx_heads):
            acc = acc + jnp.maximum(d[:, h * tq:(h + 1) * tq], 0.0) * w_ref[h:h + 1, :]
        acc = jnp.where(r0 + row_iota <= t_idx, acc, -jnp.inf)
        bits = pltpu.bitcast(acc, jnp.int32)
        key_sc[pl.ds(r0, CK), :] = jnp.where(bits < 0, bits ^ INT32_MAX, bits)

    idx_dots(0, 0)

    def score_pair(c2, carry):
        c = 2 * c2
        idx_dots(jnp.minimum(c + 1, last), 1)
        score_keys(c, 0)
        idx_dots(jnp.minimum(c + 2, last), 0)
        score_keys(jnp.minimum(c + 1, last), 1)
        return carry

    lax.fori_loop(0, (n_ch + 1) // 2, score_pair, 0)

    n_pairs = (n_ch + 1) // 2

    @pl.when(n_ch % 2 == 1)
    def _():
        key_sc[pl.ds(pl.multiple_of(n_ch * CK, CK), CK), :] = jnp.full((CK, tq), KEY_NEG_INF, jnp.int32)

    row_iota2 = lax.broadcasted_iota(jnp.int32, (2 * CK, tq), 0)

    def count_rows(pred):
        def body(c2, acc):
            r0 = pl.multiple_of(c2 * 2 * CK, 2 * CK)
            hit = jnp.where(pred(key_sc[pl.ds(r0, 2 * CK), :], r0 + row_iota2), 1, 0)
            return acc + jnp.sum(hit.reshape(2 * CK // SUBLANES, SUBLANES, tq), axis=0)
        acc = lax.fori_loop(0, n_pairs, body, jnp.zeros((SUBLANES, tq), jnp.int32))
        return jnp.sum(acc, axis=0, keepdims=True)

    thr, cut = _topk_threshold(count_rows, n_pairs * 2 * CK, topk, key_sc.shape[0], cut_sc)

    def bias_chunk(c, carry):
        r0 = pl.multiple_of(c * CK, CK)
        key = key_sc[pl.ds(r0, CK), :]
        rows = r0 + row_iota
        sel = (rows <= t_idx) & ((key > thr) | ((key == thr) & (rows <= cut)))
        bias_sc[pl.ds(r0, CK), :] = jnp.where(sel, 0.0, MASK_BIAS)
        return carry

    lax.fori_loop(0, n_ch, bias_chunk, 0)

    def tiled_bias(r0):
        return jnp.concatenate([bias_sc[pl.ds(r0, CK), :]] * n_rep, axis=1)

    def scores(g, r0):
        return jnp.dot(k_ref[g, pl.ds(r0, CK), :], q_ref[g], preferred_element_type=F32)

    def set_shift_rows(g, shift):
        pad = jnp.zeros((BF16_ROWS - 1, shift.shape[1]), F32)
        q_ref[g, hd:hd + BF16_ROWS, :] = jnp.concatenate([-shift, pad], axis=0).astype(BF16)

    acc_sc[...] = jnp.zeros(acc_sc.shape, F32)
    for g in range(n_groups):
        qf = q_ref[g, 0:hd, :].astype(F32)
        q_norm = jnp.sqrt(jnp.sum(qf * qf, axis=0, keepdims=True))
        set_shift_rows(g, q_norm * jnp.concatenate([kmax_ref[g]] * n_rep, axis=1) * (1.0 + 2.0 ** -7))

    def stage_scores(c, slot):
        cc = jnp.minimum(c, last)
        r0 = pl.multiple_of(cc * CK, CK)
        bias = jnp.where(c <= last, tiled_bias(r0), MASK_BIAS)
        for g in range(n_groups):
            s_sc[slot, g] = scores(g, r0) + bias

    def accumulate(c, slot):
        cc = jnp.minimum(c, last)
        for g in range(n_groups):
            p = jnp.exp2(s_sc[slot, g]).astype(BF16)
            acc_sc[g] = acc_sc[g] + jnp.dot(vt_ref[g, cc], p, preferred_element_type=F32)

    stage_scores(0, 0)

    def bound_pair(c2, carry):
        c = 2 * c2
        stage_scores(c + 1, 1)
        accumulate(c, 0)
        stage_scores(c + 2, 0)
        accumulate(c + 1, 1)
        return carry

    lax.fori_loop(0, (n_ch + 1) // 2, bound_pair, 0)
    for g in range(n_groups):
        l_sc[g] = acc_sc[g, hd:hd + 1, :]

    @pl.when(jnp.min(l_sc[...]) < MIN_TRUSTED_SUM)
    def _():
        m_sc[...] = jnp.full(m_sc.shape, MASK_BIAS, F32)
        l_sc[...] = jnp.zeros(l_sc.shape, F32)
        acc_sc[...] = jnp.zeros(acc_sc.shape, F32)
        for g in range(n_groups):
            set_shift_rows(g, jnp.zeros((1, n_rep * tq), F32))

        def online_chunk(c, carry):
            r0 = pl.multiple_of(c * CK, CK)
            bias = tiled_bias(r0)
            for g in range(n_groups):
                s = scores(g, r0) + bias
                m_old = m_sc[g]
                m_new = jnp.maximum(m_old, jnp.max(s, axis=0, keepdims=True))
                alpha = jnp.exp2(m_old - m_new)
                p = jnp.exp2(s - m_new)
                l_sc[g] = alpha * l_sc[g] + jnp.sum(p, axis=0, keepdims=True)
                pv = jnp.dot(vt_ref[g, c], p.astype(BF16), preferred_element_type=F32)
                acc_sc[g] = alpha * acc_sc[g] + pv
                m_sc[g] = m_new
            return carry

        lax.fori_loop(0, n_ch, online_chunk, 0)

    heads = []
    for g in range(n_groups):
        og = acc_sc[g, 0:hd, :] / l_sc[g]
        heads += [og[:, r * tq:(r + 1) * tq] for r in range(n_rep)]
    o_ref[...] = jnp.concatenate(heads, axis=0).T.astype(o_ref.dtype)


def dsa_attention_prompt(q, k, v, qi, ki, wi, topk):
    B, T, H, hd = q.shape
    G = k.shape[2]
    R = H // G
    IH, ID = qi.shape[2], qi.shape[3]
    nb = T // TQ
    assert T % (2 * CK) == 0 and T % TQ == 0, "key chunks are walked in pairs"
    qi_l = qi.astype(BF16).reshape(B, T, IH * ID)
    q_l = (q * (hd ** -0.5 * LOG2E)).astype(BF16).reshape(B, T, H * hd)
    w_l = wi.astype(F32).reshape(B, nb, TQ, IH).transpose(0, 1, 3, 2)
    k_l = k.astype(BF16).transpose(0, 2, 1, 3)
    vt_l = v.astype(BF16).reshape(B, T // CK, CK, G, hd).transpose(0, 3, 1, 4, 2)
    ki_l = ki.astype(BF16)
    kmax = jnp.sqrt(jnp.max(jnp.sum(k_l.astype(F32) ** 2, axis=-1), axis=-1))
    kmax = jnp.broadcast_to(kmax[:, :, None, None], (B, G, 1, TQ))
    ka = LANES
    k_l = jnp.concatenate([k_l, jnp.ones((B, G, T, 1), BF16), jnp.zeros((B, G, T, ka - hd - 1), BF16)], axis=-1)
    va = hd + BF16_ROWS
    vt_l = jnp.concatenate([vt_l, jnp.ones((B, G, T // CK, 1, CK), BF16),
                            jnp.zeros((B, G, T // CK, BF16_ROWS - 1, CK), BF16)], axis=3)
    kern = functools.partial(_dsa_attn_kernel, topk=topk, n_idx_heads=IH, n_groups=G, n_rep=R, hd=hd)
    return pl.pallas_call(
        kern,
        grid=(B, nb),
        in_specs=[
            pl.BlockSpec((None, TQ, IH * ID), lambda b, i: (b, i, 0)),
            pl.BlockSpec((None, None, IH, TQ), lambda b, i: (b, i, 0, 0)),
            pl.BlockSpec((None, TQ, H * hd), lambda b, i: (b, i, 0)),
            pl.BlockSpec((None, T, ID), lambda b, i: (b, 0, 0)),
            pl.BlockSpec((None, G, T, ka), lambda b, i: (b, 0, 0, 0)),
            pl.BlockSpec((None, G, T // CK, va, CK), lambda b, i: (b, 0, 0, 0, 0)),
            pl.BlockSpec((None, G, 1, TQ), lambda b, i: (b, 0, 0, 0)),
        ],
        out_specs=pl.BlockSpec((None, TQ, H * hd), lambda b, i: (b, i, 0)),
        out_shape=jax.ShapeDtypeStruct((B, T, H * hd), BF16),
        scratch_shapes=[
            pltpu.VMEM((T, TQ), jnp.int32),
            pltpu.VMEM((T, TQ), F32),
            pltpu.VMEM((1, TQ), jnp.int32),
            pltpu.VMEM((G, 1, R * TQ), F32),
            pltpu.VMEM((G, 1, R * TQ), F32),
            pltpu.VMEM((G, va, R * TQ), F32),
            pltpu.VMEM((2, CK, IH * TQ), F32),
            pltpu.VMEM((2, G, CK, R * TQ), F32),
            pltpu.VMEM((ID, IH * TQ), BF16),
            pltpu.VMEM((G, ka, R * TQ), BF16),
        ],
        compiler_params=_params("arbitrary", "arbitrary"),
        name="dsa_attention_prompt",
    )(qi_l, w_l, q_l, ki_l, k_l, vt_l, kmax)


def _final_norm_kernel(x_ref, g_ref, o_ref):
    o_ref[...] = _rms(x_ref[...], g_ref[...])


def final_rmsnorm(x, g, *, tm=512):
    M, K = x.shape
    tm = min(tm, M)
    return pl.pallas_call(
        _final_norm_kernel,
        grid=(M // tm,),
        in_specs=[pl.BlockSpec((tm, K), lambda i: (i, 0)), pl.BlockSpec((1, K), lambda i: (0, 0))],
        out_specs=pl.BlockSpec((tm, K), lambda i: (i, 0)),
        out_shape=jax.ShapeDtypeStruct((M, K), F32),
        compiler_params=_params("arbitrary"),
        name="final_rmsnorm",
    )(x, g.reshape(1, K).astype(F32))


def rope_partial(x, pos):
    d_rot = x.shape[-1] // ROPE_FRAC
    half = d_rot // 2
    inv = ROPE_THETA ** (-jnp.arange(half, dtype=F32) / half)
    ang = pos.astype(F32)[:, None] * inv[None, :]
    cos = jnp.cos(ang)[:, None, :]
    sin = jnp.sin(ang)[:, None, :]
    xr = x[..., :d_rot].astype(F32)
    x1, x2 = xr[..., :half], xr[..., half:]
    rot = jnp.concatenate([x1 * cos - x2 * sin, x2 * cos + x1 * sin], axis=-1)
    return jnp.concatenate([rot.astype(x.dtype), x[..., d_rot:]], axis=-1)


def _pad_cols(w, n):
    return jnp.pad(w, ((0, 0), (0, n - w.shape[1])))


def gla_layer(x, B, T, norm, w_in, w_a2, b_a, g_out, w_out, s0, hp):
    wdt = F32 if hp else BF16
    HK, HV = GLA_HEADS * GLA_DK, GLA_HEADS * GLA_DV
    n_pad = _round_up(w_in.shape[1], LANES)
    segs = [(0, HK), (HK, HK), (2 * HK, HV), (2 * HK + HV, HV), (2 * HK + 2 * HV, n_pad - 2 * HK - 2 * HV)]
    w2 = jnp.pad(w_a2, ((0, segs[-1][1] - GLA_RANK), (0, 0))).astype(wdt)
    q, k, v, r, g = norm_linear(x, norm, _pad_cols(w_in, n_pad).astype(wdt), segs=segs, tail=(w2, b_a),
                                hp=hp, name="gla_project")
    if s0 is None:
        sh = lambda a: a.reshape(B, T, a.shape[1])
        og, s_new = gla_scan(sh(q), sh(k), sh(g), sh(v), sh(r), g_out)
        og = og.reshape(B * T, HV)
    else:
        og, s_new = gla_step(q, k, g, v, r, g_out, s0)
    (x,) = norm_linear(og, None, w_out.astype(wdt), res=x, hp=hp, tm=512, name="gla_out_project")
    return x, s_new


def dsa_project(x, B, T, norm, w_in, pos, hp):
    wdt = F32 if hp else BF16
    QW, KW, IW = ATT_HEADS * HEAD_DIM, KV_HEADS * HEAD_DIM, IDX_HEADS * IDX_DIM
    n_pad = _round_up(w_in.shape[1], LANES)
    off_i = QW + 2 * KW + IW
    segs = [(0, QW), (QW, KW), (QW + KW, KW), (QW + 2 * KW, IW), (off_i, n_pad - off_i)]
    q, k, v, qi, kw = norm_linear(x, norm, _pad_cols(w_in, n_pad).astype(wdt), segs=segs, hp=hp, name="dsa_project")
    ki, wi = kw[:, :IDX_DIM], kw[:, IDX_DIM:IDX_DIM + IDX_HEADS]
    q = rope_partial(q.reshape(B, T, ATT_HEADS, HEAD_DIM), pos)
    k = rope_partial(k.reshape(B, T, KV_HEADS, HEAD_DIM), pos)
    v = v.reshape(B, T, KV_HEADS, HEAD_DIM)
    qi = rope_partial(qi.reshape(B, T, IDX_HEADS, IDX_DIM), pos)
    ki = rope_partial(ki.reshape(B, T, 1, IDX_DIM), pos)[:, :, 0]
    wi = wi.reshape(B, T, IDX_HEADS) * (IDX_HEADS ** -0.5 * IDX_DIM ** -0.5)
    return q, k, v, qi, ki, wi


def dsa_prompt_layer(x, B, T, norm, w_in, w_out):
    q, k, v, qi, ki, wi = dsa_project(x, B, T, norm, w_in, jnp.arange(T), False)
    o = dsa_attention_prompt(q, k, v, qi, ki, wi, min(TOPK_MAX, T // 4))
    (x,) = norm_linear(o.reshape(B * T, -1), None, w_out.astype(BF16), res=x, tm=512, name="dsa_out_project")
    return x, k, v, ki


def dsa_sample_layer(x, norm, w_in, w_out, cache_k, cache_v, cache_ki, page_table):
    DB, Tn = x.shape[0], 1
    pos = PAST_LEN + jnp.arange(Tn)
    q, k, v, qi, ki, wi = dsa_project(x, DB, Tn, norm, w_in, pos, True)
    L = PAST_LEN + Tn
    topk = min(TOPK_MAX, L // 4)
    with jax.default_matmul_precision("highest"):
        ki_past = cache_ki[page_table].reshape(DB, PAST_LEN, IDX_DIM)
        ki_all = jnp.concatenate([ki_past, ki], axis=1)
        dots = jnp.einsum('bthd,bsd->bths', qi, ki_all)
        sc = jnp.einsum('bth,bths->bts', wi, jax.nn.relu(dots))
        sc = jnp.where(jnp.arange(L)[None, None, :] <= pos[None, :, None], sc, -jnp.inf)
        idx = topk_indices(sc.reshape(DB * Tn, L), topk).reshape(DB, Tn, topk)
        valid = idx <= pos[None, :, None]
        in_past = idx < PAST_LEN
        pidx = jnp.minimum(idx, PAST_LEN - 1)
        phys = jax.vmap(lambda pt, ii: pt[ii])(page_table, pidx // PAGE_SIZE)
        row = phys * PAGE_SIZE + pidx % PAGE_SIZE
        nidx = jnp.clip(idx - PAST_LEN, 0, Tn - 1)
        sel = in_past[..., None, None]
        gather_rows = jax.vmap(lambda aa, ii: aa[ii])
        cached = lambda c: jnp.take(c.reshape(-1, KV_HEADS, HEAD_DIM), row, axis=0)
        kg = jnp.where(sel, cached(cache_k), gather_rows(k, nidx))
        vg = jnp.where(sel, cached(cache_v), gather_rows(v, nidx))
        qg = q.reshape(DB, Tn, KV_HEADS, ATT_HEADS // KV_HEADS, HEAD_DIM)
        s = jnp.einsum('btgrd,btkgd->btgrk', qg, kg) * (HEAD_DIM ** -0.5)
        s = jnp.where(valid[:, :, None, None, :], s, -jnp.inf)
        pr = jax.nn.softmax(s, axis=-1)
        o = jnp.einsum('btgrk,btkgd->btgrd', pr, vg).reshape(DB * Tn, ATT_HEADS * HEAD_DIM)
    (x,) = norm_linear(o, None, w_out, res=x, hp=True, name="dsa_out_project_s")
    return x, k, v, ki


def kernel(x_prompt, x_sample, state_gla, cache_k, cache_v, cache_idx_k, page_table,
           gla_norm, gla_w_in, gla_w_a2, gla_b_a, gla_out_norm, gla_w_out,
           dense_norm, dense_w_gate, dense_w_up, dense_w_down,
           dsa_norm, dsa_w_in, dsa_w_out,
           moe_norm, moe_w_router, moe_w_gate, moe_w_up, moe_w_down,
           final_norm):
    B, T, D = x_prompt.shape
    DB = x_sample.shape[0]
    xp = x_prompt.reshape(B * T, D)
    xs = x_sample.reshape(DB, D)
    sg_p, sg_s = [], []
    kp, vp, ip, ks_, vs_, is_ = [], [], [], [], [], []
    n_layers = gla_norm.shape[0] + dsa_norm.shape[0]
    prompt_normed = False
    for i in range(n_layers):
        j = i // 2
        if i % 2 == 0:
            xp, sp = gla_layer(xp, B, T, gla_norm[j], gla_w_in[j], gla_w_a2[j], gla_b_a[j], gla_out_norm[j],
                               gla_w_out[j], None, False)
            xs, ss = gla_layer(xs, DB, 1, gla_norm[j], gla_w_in[j], gla_w_a2[j], gla_b_a[j], gla_out_norm[j],
                               gla_w_out[j], state_gla[j], True)
            sg_p.append(sp)
            sg_s.append(ss)
            wg, wu, wd = dense_w_gate[j][None], dense_w_up[j][None], dense_w_down[j][None]
            xp = ffn(xp, dense_norm[j], wg.astype(BF16), wu.astype(BF16), wd.astype(BF16), name="dense_ffn")
            xs = ffn(xs, dense_norm[j], wg, wu, wd, hp=True, name="dense_ffn_s")
        else:
            xp, kpn, vpn, ipn = dsa_prompt_layer(xp, B, T, dsa_norm[j], dsa_w_in[j], dsa_w_out[j])
            xs, ksn, vsn, isn = dsa_sample_layer(xs, dsa_norm[j], dsa_w_in[j], dsa_w_out[j],
                                                 cache_k[j], cache_v[j], cache_idx_k[j], page_table)
            kp.append(kpn); vp.append(vpn); ip.append(ipn)
            ks_.append(ksn); vs_.append(vsn); is_.append(isn)
            wg, wu, wd = moe_w_gate[j], moe_w_up[j], moe_w_down[j]
            prompt_normed = i == n_layers - 1
            xp = moe_routed(xp, moe_norm[j], wg.astype(BF16), wu.astype(BF16), wd.astype(BF16),
                            router_gates(xp, moe_norm[j], moe_w_router[j]), final_norm if prompt_normed else None)
            xs = ffn(xs, moe_norm[j], wg, wu, wd, router_gates(xs, moe_norm[j], moe_w_router[j]),
                     hp=True, name="moe_ffn_s")
    y_prompt = (xp if prompt_normed else final_rmsnorm(xp, final_norm)).reshape(B, T, D)
    y_sample = final_rmsnorm(xs, final_norm).reshape(DB, 1, D)
    return (y_prompt, y_sample, jnp.stack(sg_p), jnp.stack(sg_s),
            jnp.stack(kp), jnp.stack(vp), jnp.stack(ip),
            jnp.stack(ks_), jnp.stack(vs_), jnp.stack(is_))
```
